```python
import math
import jax, jax.numpy as jnp
from jax import lax
import numpy as np

D_MODEL = 1024
BATCH = 8
SEQ = 2048
DEPTH = 4
DEC_BATCH = 8
DEC_SEQ = 16
PAST_LEN = 4096

CHUNK = 64
Q_BLOCK = 128
HD_A = 64
H_A = D_MODEL // (2 * HD_A)
DH_B = 64
H_B = D_MODEL // (4 * DH_B)
W_A = H_A * HD_A
W_B = H_B * 2 * DH_B
D_FF = -(-8 * D_MODEL // (3 * 256)) * 256
PLE_DIM = 256
IN_SPLITS = (W_A, W_A, W_A, H_A, W_B, W_B, W_B, D_MODEL, D_MODEL)
IN_COLS = 3 * W_A + H_A + 3 * W_B + 2 * D_MODEL
ALPHA = (2 * DEPTH) ** 0.25
BETA = (8 * DEPTH) ** -0.25
LN_EPS = 1e-5
RMS_EPS = 1e-5
NEG_INF = -1e30

kernel_name = 'fox_diff_hybrid_stream_encoder'


def _split_cols(a, sizes):
    offs = []
    acc = 0
    for s in sizes[:-1]:
        acc += s
        offs.append(acc)
    return jnp.split(a, offs, axis=-1)


def _layer_norm(x, g, b):
    xf = x.astype(jnp.float32)
    mu = jnp.mean(xf, axis=-1, keepdims=True)
    xc = xf - mu
    var = jnp.mean(xc * xc, axis=-1, keepdims=True)
    y = xc * lax.rsqrt(var + LN_EPS) * g.astype(jnp.float32) + b.astype(jnp.float32)
    return y.astype(x.dtype)


def _sweep_queries(block_fn, q_arrays, q_pos):
    t_q = q_pos.shape[0]
    if t_q <= Q_BLOCK:
        return block_fn(q_arrays, q_pos)
    n_blk = t_q // Q_BLOCK
    blocked = tuple(jnp.moveaxis(a.reshape(a.shape[:1] + (n_blk, Q_BLOCK) + a.shape[2:]), 1, 0)
                    for a in q_arrays)
    out = lax.map(lambda xs: block_fn(xs[0], xs[1]), (blocked, q_pos.reshape(n_blk, Q_BLOCK)))
    out = jnp.moveaxis(out, 0, 1)
    return out.reshape(out.shape[:1] + (t_q,) + out.shape[3:])


def _fox_attention(q, k, v, c_q, c_k, q_pos, k_pos):
    scale = HD_A ** -0.5
    c_k_t = jnp.swapaxes(c_k, 1, 2)[:, :, None, :]

    def block(qa, qp):
        qb, cq = qa
        s = jnp.einsum('bqhd,bkhd->bhqk', qb, k).astype(jnp.float32) * scale
        s = s + jnp.swapaxes(cq, 1, 2)[..., None] - c_k_t
        mask = k_pos[None, :] <= qp[:, None]
        s = jnp.where(mask, s, NEG_INF)
        p = jax.nn.softmax(s, axis=-1)
        return jnp.einsum('bhqk,bkhd->bqhd', p.astype(v.dtype), v)

    return _sweep_queries(block, (q, c_q), q_pos)


def _diff_attention(q, k, v, lam, q_pos, k_pos):
    scale = DH_B ** -0.5
    slopes = 2.0 ** (-8.0 * jnp.arange(1, H_B + 1, dtype=jnp.float32) / H_B)
    k_chunk = k_pos // CHUNK

    def block(qa, qp):
        (qb,) = qa
        s = jnp.einsum('bqhcd,bkhcd->bchqk', qb, k).astype(jnp.float32) * scale
        dist = jnp.abs(qp[:, None] - k_pos[None, :]).astype(jnp.float32)
        s = s - slopes[:, None, None] * dist[None]
        mask = k_chunk[None, :] <= (qp // CHUNK)[:, None]
        s = jnp.where(mask, s, NEG_INF)
        p = jax.nn.softmax(s, axis=-1)
        a = p[:, 0] - lam * p[:, 1]
        return jnp.einsum('bhqk,bkhe->bqhe', a.astype(v.dtype), v)

    return _sweep_queries(block, (q,), q_pos)


def _hybrid_layer(x, p, past, q_pos, k_pos, lambda_init,
                  w_in, b_f, lq1, lk1, lq2, lk2, g_diff, w_ba, w_bb, w_o, ln1_g, ln1_b,
                  w_g, w_u, w_d, w_pg, w_pp, ln2_g, ln2_b):
    bsz, t, _ = x.shape
    proj = x @ w_in
    qa, ka, va, fa, qb, kb, vb, ga, gb = _split_cols(proj, IN_SPLITS)
    qa = qa.reshape(bsz, t, H_A, HD_A)
    ka = ka.reshape(bsz, t, H_A, HD_A)
    va = va.reshape(bsz, t, H_A, HD_A)
    logf = jax.nn.log_sigmoid((fa + b_f).astype(jnp.float32))
    qb = qb.reshape(bsz, t, H_B, 2, DH_B)
    kb = kb.reshape(bsz, t, H_B, 2, DH_B)
    vb = vb.reshape(bsz, t, H_B, 2 * DH_B)

    if past is None:
        ka_all, va_all, logf_all, kb_all, vb_all = ka, va, logf, kb, vb
    else:
        pk, pv, plf, pkb, pvb = past
        ka_all = jnp.concatenate([pk, ka], axis=1)
        va_all = jnp.concatenate([pv, va], axis=1)
        logf_all = jnp.concatenate([plf.astype(jnp.float32), logf], axis=1)
        kb_all = jnp.concatenate([pkb, kb], axis=1)
        vb_all = jnp.concatenate([pvb, vb], axis=1)

    c_all = jnp.cumsum(logf_all, axis=1)
    c_q = c_all[:, -t:]
    oa = _fox_attention(qa, ka_all, va_all, c_q, c_all, q_pos, k_pos)

    f32 = jnp.float32
    lam = (jnp.exp(jnp.sum(lq1.astype(f32) * lk1.astype(f32)))
           - jnp.exp(jnp.sum(lq2.astype(f32) * lk2.astype(f32))) + lambda_init)
    ob = _diff_attention(qb, kb_all, vb_all, lam, q_pos, k_pos).astype(f32)
    ob = ob * lax.rsqrt(jnp.mean(ob * ob, axis=-1, keepdims=True) + RMS_EPS)
    ob = (ob * g_diff.astype(f32).reshape(H_B, 2 * DH_B) * (1.0 - lambda_init)).astype(x.dtype)

    merged = (jax.nn.sigmoid(ga) * (oa.reshape(bsz, t, W_A) @ w_ba)
              + jax.nn.sigmoid(gb) * (ob.reshape(bsz, t, W_B) @ w_bb))
    x = _layer_norm(ALPHA * x + merged @ w_o, ln1_g, ln1_b)

    ffn = (jax.nn.silu(x @ w_g) * (x @ w_u)) @ w_d
    ple = jax.nn.sigmoid(x @ w_pg) * (p @ w_pp)
    x = _layer_norm(ALPHA * x + ffn + ple, ln2_g, ln2_b)
    return x, (ka, va, logf, kb, vb)


def setup_inputs(seed: int = 0) -> dict:
    key = jax.random.key(seed)
    ks = jax.random.split(key, 32)
    f32 = jnp.float32
    nrm = lambda k, shape, s: jax.random.normal(k, shape, f32) * s
    return {
        'x_prompt': nrm(ks[0], (BATCH, SEQ, D_MODEL), 1.0),
        'x_sample': nrm(ks[1], (DEC_BATCH, DEC_SEQ, D_MODEL), 1.0),
        'p_prompt': nrm(ks[2], (DEPTH, BATCH, SEQ, PLE_DIM), 1.0),
        'p_sample': nrm(ks[3], (DEPTH, DEC_BATCH, DEC_SEQ, PLE_DIM), 1.0),
        'cache_fox_k': nrm(ks[4], (DEPTH, DEC_BATCH, PAST_LEN, H_A, HD_A), 1.0),
        'cache_fox_v': nrm(ks[5], (DEPTH, DEC_BATCH, PAST_LEN, H_A, HD_A), 1.0),
        'cache_fox_logf': jax.nn.log_sigmoid(
            jax.random.uniform(ks[6], (DEPTH, DEC_BATCH, PAST_LEN, H_A), f32, 1.0, 4.0)),
        'cache_diff_k': nrm(ks[7], (DEPTH, DEC_BATCH, PAST_LEN, H_B, 2, DH_B), 1.0),
        'cache_diff_v': nrm(ks[8], (DEPTH, DEC_BATCH, PAST_LEN, H_B, 2 * DH_B), 1.0),
        'w_in': nrm(ks[9], (DEPTH, D_MODEL, IN_COLS), D_MODEL ** -0.5),
        'b_forget': jax.random.uniform(ks[10], (DEPTH, H_A), f32, 1.0, 4.0),
        'lambda_q1': nrm(ks[11], (DEPTH, DH_B), 0.1),
        'lambda_k1': nrm(ks[12], (DEPTH, DH_B), 0.1),
        'lambda_q2': nrm(ks[13], (DEPTH, DH_B), 0.1),
        'lambda_k2': nrm(ks[14], (DEPTH, DH_B), 0.1),
        'diff_norm_g': 1.0 + nrm(ks[15], (DEPTH, W_B), 0.02),
        'w_branch_fox': nrm(ks[16], (DEPTH, W_A, D_MODEL), BETA * W_A ** -0.5),
        'w_branch_diff': nrm(ks[17], (DEPTH, W_B, D_MODEL), BETA * W_B ** -0.5),
        'w_out': nrm(ks[18], (DEPTH, D_MODEL, D_MODEL), BETA * D_MODEL ** -0.5),
        'ln1_g': 1.0 + nrm(ks[19], (DEPTH, D_MODEL), 0.02),
        'ln1_b': nrm(ks[20], (DEPTH, D_MODEL), 0.02),
        'w_ffn_gate': nrm(ks[21], (DEPTH, D_MODEL, D_FF), D_MODEL ** -0.5),
        'w_ffn_up': nrm(ks[22], (DEPTH, D_MODEL, D_FF), D_MODEL ** -0.5),
        'w_ffn_down': nrm(ks[23], (DEPTH, D_FF, D_MODEL), BETA * D_FF ** -0.5),
        'w_ple_gate': nrm(ks[24], (DEPTH, D_MODEL, D_MODEL), D_MODEL ** -0.5),
        'w_ple_proj': nrm(ks[25], (DEPTH, PLE_DIM, D_MODEL), BETA * PLE_DIM ** -0.5),
        'ln2_g': 1.0 + nrm(ks[26], (DEPTH, D_MODEL), 0.02),
        'ln2_b': nrm(ks[27], (DEPTH, D_MODEL), 0.02),
    }


def reference(x_prompt, x_sample, p_prompt, p_sample, cache_fox_k, cache_fox_v, cache_fox_logf,
              cache_diff_k, cache_diff_v, w_in, b_forget, lambda_q1, lambda_k1, lambda_q2, lambda_k2,
              diff_norm_g, w_branch_fox, w_branch_diff, w_out, ln1_g, ln1_b, w_ffn_gate, w_ffn_up,
              w_ffn_down, w_ple_gate, w_ple_proj, ln2_g, ln2_b):
    t_p = x_prompt.shape[1]
    t_s = x_sample.shape[1]
    past_len = cache_fox_k.shape[2]
    pos_p = jnp.arange(t_p, dtype=jnp.int32)
    q_pos_s = past_len + jnp.arange(t_s, dtype=jnp.int32)
    k_pos_s = jnp.arange(past_len + t_s, dtype=jnp.int32)

    hp, hs = x_prompt, x_sample
    new_p = ([], [], [], [], [])
    new_s = ([], [], [], [], [])
    for l in range(DEPTH):
        lambda_init = 0.8 - 0.6 * math.exp(-0.3 * l)
        lw = (w_in[l], b_forget[l], lambda_q1[l], lambda_k1[l], lambda_q2[l], lambda_k2[l],
              diff_norm_g[l], w_branch_fox[l], w_branch_diff[l], w_out[l], ln1_g[l], ln1_b[l],
              w_ffn_gate[l], w_ffn_up[l], w_ffn_down[l], w_ple_gate[l], w_ple_proj[l],
              ln2_g[l], ln2_b[l])
        hp, rows_p = _hybrid_layer(hp, p_prompt[l], None, pos_p, pos_p, lambda_init, *lw)
        past = (cache_fox_k[l], cache_fox_v[l], cache_fox_logf[l], cache_diff_k[l], cache_diff_v[l])
        hs, rows_s = _hybrid_layer(hs, p_sample[l], past, q_pos_s, k_pos_s, lambda_init, *lw)
        for lst, r in zip(new_p, rows_p):
            lst.append(r)
        for lst, r in zip(new_s, rows_s):
            lst.append(r)

    fox_k_p, fox_v_p, fox_lf_p, diff_k_p, diff_v_p = [jnp.stack(a) for a in new_p]
    fox_k_s, fox_v_s, fox_lf_s, diff_k_s, diff_v_s = [jnp.stack(a) for a in new_s]
    return (hp, hs, fox_k_p, fox_v_p, fox_lf_p, diff_k_p, diff_v_p,
            fox_k_s, fox_v_s, fox_lf_s, diff_k_s, diff_v_s)
```

```python
import functools
import math

import jax
import jax.numpy as jnp
from jax import lax
from jax.experimental import pallas as pl
from jax.experimental.pallas import tpu as pltpu

F32 = jnp.float32
BF16 = jnp.bfloat16

HD_A = 64
DH_B = 64
CHUNK = 64
LN_EPS = 1e-5
RMS_EPS = 1e-5
NEG_INF = -1e30
LANES = 128
VMEM_LIMIT = 56 * 1024 * 1024


def _cparams(n_axes):
    return pltpu.CompilerParams(dimension_semantics=("arbitrary",) * n_axes,
                                vmem_limit_bytes=VMEM_LIMIT)


def _const_spec(shape):
    zeros = (0,) * len(shape)
    return pl.BlockSpec(shape, lambda *_: zeros, pipeline_mode=pl.Buffered(1))


def _dot(a, b):
    return jnp.dot(a, b, preferred_element_type=F32)


def _dot_nt(a, b):
    return lax.dot_general(a, b, (((1,), (1,)), ((), ())), preferred_element_type=F32)


def _sigmoid(x):
    return 1.0 / (1.0 + jnp.exp(-x))


def _layer_norm(y, g, b):
    mu = jnp.mean(y, axis=-1, keepdims=True)
    yc = y - mu
    var = jnp.mean(yc * yc, axis=-1, keepdims=True)
    return yc * lax.rsqrt(var + LN_EPS) * g + b


def _inproj_kernel(x_ref, w_ref, wf_ref, bf_ref,
                   qa_ref, qb_ref, ka_ref, va_ref, kb_ref, vb_ref,
                   kab_ref, vab_ref, kbb_ref, vbb_ref, logf_ref, *, width, n_heads):
    xb = x_ref[...].astype(BF16)

    def proj(i):
        return _dot(xb, w_ref[:, i * width:(i + 1) * width])

    scale = HD_A ** -0.5
    qa_ref[...] = (proj(0) * scale).astype(BF16)
    ka = proj(1)
    ka_ref[...] = ka
    kab_ref[...] = ka.astype(BF16)
    va = proj(2)
    va_ref[...] = va
    vab_ref[...] = va.astype(BF16)
    qb_ref[...] = (proj(3) * (DH_B ** -0.5)).astype(BF16)
    kb = proj(4)
    kb_ref[...] = kb
    kbb_ref[...] = kb.astype(BF16)
    vb = proj(5)
    vb_ref[...] = vb
    vbb_ref[...] = vb.astype(BF16)
    z = _dot(xb, wf_ref[...]) + bf_ref[...]
    logf = jnp.minimum(z, 0.0) - jnp.log1p(jnp.exp(-jnp.abs(z)))
    logf_ref[...] = logf[:, :n_heads]


def _inproj(x, w_qkv, w_f, b_f, *, tm, n_heads):
    m, d = x.shape
    width = w_qkv.shape[1] // 6
    row = lambda i: (i, 0)
    blk = pl.BlockSpec((tm, width), row)
    out_shape = ([jax.ShapeDtypeStruct((m, width), BF16)] * 2
                 + [jax.ShapeDtypeStruct((m, width), F32)] * 4
                 + [jax.ShapeDtypeStruct((m, width), BF16)] * 4
                 + [jax.ShapeDtypeStruct((m, n_heads), F32)])
    return pl.pallas_call(
        functools.partial(_inproj_kernel, width=width, n_heads=n_heads),
        grid=(m // tm,),
        in_specs=[pl.BlockSpec((tm, d), row), _const_spec(w_qkv.shape), _const_spec(w_f.shape),
                  _const_spec(b_f.shape)],
        out_specs=[blk] * 10 + [pl.BlockSpec((tm, n_heads), row)],
        out_shape=out_shape,
        compiler_params=_cparams(1),
        name="inproj",
    )(x, w_qkv, w_f, b_f)


SCAN_BLOCK = 256


def _split3(a):
    a1 = a.astype(BF16)
    r1 = a - a1.astype(F32)
    a2 = r1.astype(BF16)
    a3 = (r1 - a2.astype(F32)).astype(BF16)
    return a1, a2, a3


def _cumsum_kernel(lf_ref, c_ref, *, n_blk):
    rows = lf_ref.shape[0]
    r = lax.broadcasted_iota(jnp.int32, (SCAN_BLOCK, SCAN_BLOCK), 0)
    c = lax.broadcasted_iota(jnp.int32, (SCAN_BLOCK, SCAN_BLOCK), 1)
    upper = jnp.where(r <= c, 1.0, 0.0).astype(BF16)
    carry = jnp.zeros((rows, 1), F32)
    for j in range(n_blk):
        sl = slice(j * SCAN_BLOCK, (j + 1) * SCAN_BLOCK)
        a1, a2, a3 = _split3(lf_ref[:, sl])
        blk = _dot(a1, upper) + _dot(a2, upper) + _dot(a3, upper) + carry
        c_ref[:, sl] = blk
        carry = blk[:, SCAN_BLOCK - 1:SCAN_BLOCK]


def _cumsum_rows(lf_rows):
    rows, t = lf_rows.shape
    assert t % SCAN_BLOCK == 0
    return pl.pallas_call(
        functools.partial(_cumsum_kernel, n_blk=t // SCAN_BLOCK),
        out_shape=jax.ShapeDtypeStruct((rows, t), F32),
        compiler_params=pltpu.CompilerParams(vmem_limit_bytes=VMEM_LIMIT),
        name="logf_cumsum",
    )(lf_rows)


def _half_masks(rows):
    lane = lax.broadcasted_iota(jnp.int32, (rows, LANES), 1)
    lo = jnp.where(lane < HD_A, 1.0, 0.0)
    return lo, 1.0 - lo


def _masked_q(q2, mask):
    return (q2.astype(F32) * mask).astype(BF16)


def _flash_init(m_ref, l_ref, acc_ref):
    m_ref[...] = jnp.full(m_ref.shape, NEG_INF, F32)
    l_ref[...] = jnp.zeros(l_ref.shape, F32)
    acc_ref[...] = jnp.zeros(acc_ref.shape, F32)


def _flash_step(s, v2, m_ref, l_ref, acc_ref):
    m_prev = m_ref[...]
    m_new = jnp.maximum(m_prev, jnp.max(s, axis=1, keepdims=True))
    alpha = jnp.exp(m_prev - m_new)
    p = jnp.exp(s - m_new)
    l_ref[...] = alpha * l_ref[...] + jnp.sum(p, axis=1, keepdims=True)
    acc_ref[...] = alpha * acc_ref[...] + _dot(p.astype(BF16), v2)
    m_ref[...] = m_new


def _lambda_scalar(lq1, lk1, lq2, lk2, lambda_init):
    return (jnp.exp(jnp.sum(lq1 * lk1, axis=1, keepdims=True))
            - jnp.exp(jnp.sum(lq2 * lk2, axis=1, keepdims=True)) + lambda_init)


def _diff_finish(acc1, l1, acc2, l2, lam, g_row, lambda_init):
    o = acc1 / l1 - lam * (acc2 / l2)
    o = o * lax.rsqrt(jnp.mean(o * o, axis=-1, keepdims=True) + RMS_EPS)
    return o * g_row * (1.0 - lambda_init)


def _alibi_slope(h, n_heads):
    return 2.0 ** (-8.0 * (h + 1) / n_heads)


def _fox_prompt_kernel(q_ref, k_ref, v_ref, ccol_ref, crow_ref, o_ref, m_ref, l_ref, acc_ref,
                       *, tq, n_heads):
    qi = pl.program_id(1)
    lo, hi = _half_masks(tq)
    q_pos = lax.broadcasted_iota(jnp.int32, (tq, tq), 0)
    k_pos = lax.broadcasted_iota(jnp.int32, (tq, tq), 1)
    causal = k_pos <= q_pos
    for pair in range(n_heads // 2):
        lanes = slice(pair * LANES, (pair + 1) * LANES)
        q2 = q_ref[:, lanes]
        pair_out = []
        for half, mask in enumerate((lo, hi)):
            h = 2 * pair + half
            qm = _masked_q(q2, mask)
            cq = ccol_ref[0, :, h:h + 1]
            _flash_init(m_ref, l_ref, acc_ref)

            def scores(j):
                ks = pl.multiple_of(j * tq, tq)
                s = _dot_nt(qm, k_ref[pl.ds(ks, tq), lanes])
                s = s + cq - crow_ref[0, h:h + 1, pl.ds(ks, tq)]
                return s, v_ref[pl.ds(ks, tq), lanes]

            def full_block(j, carry):
                s, v2 = scores(j)
                _flash_step(s, v2, m_ref, l_ref, acc_ref)
                return carry

            lax.fori_loop(0, qi, full_block, 0)
            s, v2 = scores(qi)
            _flash_step(jnp.where(causal, s, NEG_INF), v2, m_ref, l_ref, acc_ref)
            pair_out.append(acc_ref[...] / l_ref[...])
        o_ref[:, lanes] = (pair_out[0] * lo + pair_out[1] * hi).astype(BF16)


def _fox_prompt(q, k, v, c_col, c_row, *, batch, seq, tq, n_heads):
    width = n_heads * HD_A
    nq = seq // tq
    return pl.pallas_call(
        functools.partial(_fox_prompt_kernel, tq=tq, n_heads=n_heads),
        grid=(batch, nq),
        in_specs=[pl.BlockSpec((tq, width), lambda b, i: (b * nq + i, 0)),
                  pl.BlockSpec((seq, width), lambda b, i: (b, 0)),
                  pl.BlockSpec((seq, width), lambda b, i: (b, 0)),
                  pl.BlockSpec((1, tq, n_heads), lambda b, i: (b, i, 0)),
                  pl.BlockSpec((1, n_heads, seq), lambda b, i: (b, 0, 0))],
        out_specs=pl.BlockSpec((tq, width), lambda b, i: (b * nq + i, 0)),
        out_shape=jax.ShapeDtypeStruct((batch * seq, width), BF16),
        scratch_shapes=[pltpu.VMEM((tq, 1), F32), pltpu.VMEM((tq, 1), F32),
                        pltpu.VMEM((tq, LANES), F32)],
        compiler_params=_cparams(2),
        name="fox_prompt",
    )(q, k, v, c_col, c_row)


def _diff_prompt_kernel(q_ref, k_ref, v_ref, lq1_ref, lk1_ref, lq2_ref, lk2_ref, g_ref, o_ref,
                        m_ref, l_ref, acc_ref, *, tq, n_heads, lambda_init):
    qi = pl.program_id(1)
    lo, hi = _half_masks(tq)
    lam = _lambda_scalar(lq1_ref[...], lk1_ref[...], lq2_ref[...], lk2_ref[...], lambda_init)
    q_loc = lax.broadcasted_iota(jnp.int32, (tq, tq), 0)
    k_loc = lax.broadcasted_iota(jnp.int32, (tq, tq), 1)
    visible = (k_loc // CHUNK) <= (q_loc // CHUNK)
    dist = jnp.abs(q_loc - k_loc).astype(F32)
    q_abs = (qi * tq + lax.broadcasted_iota(jnp.int32, (tq, 1), 0)).astype(F32)
    k_rel = lax.broadcasted_iota(jnp.int32, (1, tq), 1).astype(F32)
    for h in range(n_heads):
        lanes = slice(h * LANES, (h + 1) * LANES)
        slope = _alibi_slope(h, n_heads)
        q2 = q_ref[:, lanes]
        finals = []
        for comp, mask in enumerate((lo, hi)):
            qm = _masked_q(q2, mask)
            _flash_init(m_ref, l_ref, acc_ref)

            def full_block(j, carry):
                ks = pl.multiple_of(j * tq, tq)
                s = _dot_nt(qm, k_ref[pl.ds(ks, tq), lanes])
                k_abs = (j * tq).astype(F32) + k_rel
                s = s - slope * q_abs + slope * k_abs
                _flash_step(s, v_ref[pl.ds(ks, tq), lanes], m_ref, l_ref, acc_ref)
                return carry

            lax.fori_loop(0, qi, full_block, 0)
            ks = pl.multiple_of(qi * tq, tq)
            s = _dot_nt(qm, k_ref[pl.ds(ks, tq), lanes]) - slope * dist
            _flash_step(jnp.where(visible, s, NEG_INF), v_ref[pl.ds(ks, tq), lanes],
                        m_ref, l_ref, acc_ref)
            finals.append((acc_ref[...], l_ref[...]))
        (acc1, l1), (acc2, l2) = finals
        o = _diff_finish(acc1, l1, acc2, l2, lam, g_ref[:, lanes], lambda_init)
        o_ref[:, lanes] = o.astype(BF16)


def _diff_prompt(q, k, v, lams, g, *, batch, seq, tq, n_heads, lambda_init):
    width = n_heads * 2 * DH_B
    nq = seq // tq
    lam_specs = [_const_spec(a.shape) for a in lams]
    return pl.pallas_call(
        functools.partial(_diff_prompt_kernel, tq=tq, n_heads=n_heads, lambda_init=lambda_init),
        grid=(batch, nq),
        in_specs=[pl.BlockSpec((tq, width), lambda b, i: (b * nq + i, 0)),
                  pl.BlockSpec((seq, width), lambda b, i: (b, 0)),
                  pl.BlockSpec((seq, width), lambda b, i: (b, 0))]
                 + lam_specs + [_const_spec(g.shape)],
        out_specs=pl.BlockSpec((tq, width), lambda b, i: (b * nq + i, 0)),
        out_shape=jax.ShapeDtypeStruct((batch * seq, width), BF16),
        scratch_shapes=[pltpu.VMEM((tq, 1), F32), pltpu.VMEM((tq, 1), F32),
                        pltpu.VMEM((tq, LANES), F32)],
        compiler_params=_cparams(2),
        name="diff_prompt",
    )(q, k, v, *lams, g)


def _fox_decode_kernel(q_ref, kc_ref, vc_ref, kn_ref, vn_ref, ccol_ref, crow_ref, o_ref,
                       m_ref, l_ref, acc_ref, *, tq, tk, n_kv, n_heads):
    j = pl.program_id(1)
    lo, hi = _half_masks(tq)

    @pl.when(j == 0)
    def _():
        _flash_init(m_ref, l_ref, acc_ref)

    def head_loop(k_of, v_of, bias_of, mask):
        for pair in range(n_heads // 2):
            lanes = slice(pair * LANES, (pair + 1) * LANES)
            q2 = q_ref[:, lanes]
            k2 = k_of(lanes)
            v2 = v_of(lanes)
            for half, hm in enumerate((lo, hi)):
                h = 2 * pair + half
                s = _dot_nt(_masked_q(q2, hm), k2) + bias_of(h)
                if mask is not None:
                    s = jnp.where(mask, s, NEG_INF)
                _flash_step(s, v2, m_ref.at[h], l_ref.at[h], acc_ref.at[h])

    @pl.when(j < n_kv)
    def _():
        ks = pl.multiple_of(j * tk, tk)
        head_loop(lambda lanes: kc_ref[0, :, lanes].astype(BF16),
                  lambda lanes: vc_ref[0, :, lanes].astype(BF16),
                  lambda h: ccol_ref[0, :, h:h + 1] - crow_ref[0, h:h + 1, pl.ds(ks, tk)],
                  None)

    @pl.when(j == n_kv)
    def _():
        past = n_kv * tk
        q_pos = lax.broadcasted_iota(jnp.int32, (tq, tq), 0)
        k_pos = lax.broadcasted_iota(jnp.int32, (tq, tq), 1)
        head_loop(lambda lanes: kn_ref[:, lanes],
                  lambda lanes: vn_ref[:, lanes],
                  lambda h: ccol_ref[0, :, h:h + 1] - crow_ref[0, h:h + 1, past:past + tq],
                  k_pos <= q_pos)
        for pair in range(n_heads // 2):
            lanes = slice(pair * LANES, (pair + 1) * LANES)
            o0 = acc_ref[2 * pair] / l_ref[2 * pair]
            o1 = acc_ref[2 * pair + 1] / l_ref[2 * pair + 1]
            o_ref[:, lanes] = (o0 * lo + o1 * hi).astype(BF16)


def _fox_decode(q, k_cache, v_cache, k_new, v_new, c_col, c_row, *, batch, tq, tk, row0, n_heads):
    width = n_heads * HD_A
    past = k_cache.shape[1]
    n_kv = past // tk
    new_blk = lambda b, j: (row0 // tq + b, 0)
    cache_blk = lambda b, j: (b, jnp.minimum(j, n_kv - 1), 0)
    return pl.pallas_call(
        functools.partial(_fox_decode_kernel, tq=tq, tk=tk, n_kv=n_kv, n_heads=n_heads),
        grid=(batch, n_kv + 1),
        in_specs=[pl.BlockSpec((tq, width), new_blk),
                  pl.BlockSpec((1, tk, width), cache_blk),
                  pl.BlockSpec((1, tk, width), cache_blk),
                  pl.BlockSpec((tq, width), new_blk),
                  pl.BlockSpec((tq, width), new_blk),
                  pl.BlockSpec((1, tq, n_heads), lambda b, j: (b, past // tq, 0)),
                  pl.BlockSpec((1, n_heads, c_row.shape[2]), lambda b, j: (b, 0, 0))],
        out_specs=pl.BlockSpec((tq, width), lambda b, j: (b, 0)),
        out_shape=jax.ShapeDtypeStruct((batch * tq, width), BF16),
        scratch_shapes=[pltpu.VMEM((n_heads, tq, 1), F32), pltpu.VMEM((n_heads, tq, 1), F32),
                        pltpu.VMEM((n_heads, tq, LANES), F32)],
        compiler_params=_cparams(2),
        name="fox_decode",
    )(q, k_cache, v_cache, k_new, v_new, c_col, c_row)


def _diff_decode_kernel(q_ref, kc_ref, vc_ref, kn_ref, vn_ref, lq1_ref, lk1_ref, lq2_ref, lk2_ref,
                        g_ref, o_ref, m_ref, l_ref, acc_ref,
                        *, tq, tk, n_kv, n_heads, lambda_init):
    j = pl.program_id(1)
    lo, hi = _half_masks(tq)
    past = n_kv * tk
    q_abs = (past + lax.broadcasted_iota(jnp.int32, (tq, 1), 0)).astype(F32)

    @pl.when(j == 0)
    def _():
        _flash_init(m_ref, l_ref, acc_ref)

    def head_loop(k_of, v_of, dist):
        for h in range(n_heads):
            lanes = slice(h * LANES, (h + 1) * LANES)
            slope = _alibi_slope(h, n_heads)
            q2 = q_ref[:, lanes]
            k2 = k_of(lanes)
            v2 = v_of(lanes)
            for comp, hm in enumerate((lo, hi)):
                s = _dot_nt(_masked_q(q2, hm), k2) - slope * dist
                idx = 2 * h + comp
                _flash_step(s, v2, m_ref.at[idx], l_ref.at[idx], acc_ref.at[idx])

    @pl.when(j < n_kv)
    def _():
        k_abs = (j * tk + lax.broadcasted_iota(jnp.int32, (1, tk), 1)).astype(F32)
        head_loop(lambda lanes: kc_ref[0, :, lanes].astype(BF16),
                  lambda lanes: vc_ref[0, :, lanes].astype(BF16),
                  jnp.abs(q_abs - k_abs))

    @pl.when(j == n_kv)
    def _():
        assert past % CHUNK == 0 and tq <= CHUNK
        k_abs = (past + lax.broadcasted_iota(jnp.int32, (1, tq), 1)).astype(F32)
        head_loop(lambda lanes: kn_ref[:, lanes], lambda lanes: vn_ref[:, lanes],
                  jnp.abs(q_abs - k_abs))
        lam = _lambda_scalar(lq1_ref[...], lk1_ref[...], lq2_ref[...], lk2_ref[...], lambda_init)
        for h in range(n_heads):
            lanes = slice(h * LANES, (h + 1) * LANES)
            o = _diff_finish(acc_ref[2 * h], l_ref[2 * h], acc_ref[2 * h + 1], l_ref[2 * h + 1],
                             lam, g_ref[:, lanes], lambda_init)
            o_ref[:, lanes] = o.astype(BF16)


def _diff_decode(q, k_cache, v_cache, k_new, v_new, lams, g, *, batch, tq, tk, row0, n_heads,
                 lambda_init):
    width = n_heads * 2 * DH_B
    past = k_cache.shape[1]
    n_kv = past // tk
    new_blk = lambda b, j: (row0 // tq + b, 0)
    cache_blk = lambda b, j: (b, jnp.minimum(j, n_kv - 1), 0)
    return pl.pallas_call(
        functools.partial(_diff_decode_kernel, tq=tq, tk=tk, n_kv=n_kv, n_heads=n_heads,
                          lambda_init=lambda_init),
        grid=(batch, n_kv + 1),
        in_specs=[pl.BlockSpec((tq, width), new_blk),
                  pl.BlockSpec((1, tk, width), cache_blk),
                  pl.BlockSpec((1, tk, width), cache_blk),
                  pl.BlockSpec((tq, width), new_blk),
                  pl.BlockSpec((tq, width), new_blk)]
                 + [_const_spec(a.shape) for a in lams] + [_const_spec(g.shape)],
        out_specs=pl.BlockSpec((tq, width), lambda b, j: (b, 0)),
        out_shape=jax.ShapeDtypeStruct((batch * tq, width), BF16),
        scratch_shapes=[pltpu.VMEM((2 * n_heads, tq, 1), F32),
                        pltpu.VMEM((2 * n_heads, tq, 1), F32),
                        pltpu.VMEM((2 * n_heads, tq, LANES), F32)],
        compiler_params=_cparams(2),
        name="diff_decode",
    )(q, k_cache, v_cache, k_new, v_new, *lams, g)


def _post_kernel(x_ref, oa_ref, ob_ref, wg_ref, wba_ref, wbb_ref, wo_ref, g_ref, b_ref, o_ref,
                 *, alpha):
    x = x_ref[...]
    xb = x.astype(BF16)
    d = x.shape[1]
    ga = _dot(xb, wg_ref[:, :d])
    merged = _sigmoid(ga) * _dot(oa_ref[...], wba_ref[...])
    gb = _dot(xb, wg_ref[:, d:])
    merged = merged + _sigmoid(gb) * _dot(ob_ref[...], wbb_ref[...])
    y = alpha * x + _dot(merged.astype(BF16), wo_ref[...])
    o_ref[...] = _layer_norm(y, g_ref[...], b_ref[...])


def _post(x, oa, ob, w_gate, w_ba, w_bb, w_o, g, b, *, tm, alpha):
    m, d = x.shape
    row = lambda i: (i, 0)
    return pl.pallas_call(
        functools.partial(_post_kernel, alpha=alpha),
        grid=(m // tm,),
        in_specs=[pl.BlockSpec((tm, d), row), pl.BlockSpec((tm, oa.shape[1]), row),
                  pl.BlockSpec((tm, ob.shape[1]), row)]
                 + [_const_spec(a.shape) for a in (w_gate, w_ba, w_bb, w_o, g, b)],
        out_specs=pl.BlockSpec((tm, d), row),
        out_shape=jax.ShapeDtypeStruct((m, d), F32),
        compiler_params=_cparams(1),
        name="merge_outproj_ln",
    )(x, oa, ob, w_gate, w_ba, w_bb, w_o, g, b)


def _ffn_kernel(x_ref, p_ref, wg_ref, wu_ref, wd_ref, wpg_ref, wpp_ref, g_ref, b_ref, o_ref,
                *, alpha):
    x = x_ref[...]
    xb = x.astype(BF16)
    hg = _dot(xb, wg_ref[...])
    hidden = (hg * _sigmoid(hg) * _dot(xb, wu_ref[...])).astype(BF16)
    y = alpha * x + _dot(hidden, wd_ref[...])
    ple = _sigmoid(_dot(xb, wpg_ref[...])) * _dot(p_ref[...].astype(BF16), wpp_ref[...])
    o_ref[...] = _layer_norm(y + ple, g_ref[...], b_ref[...])


def _ffn(x, p, w_g, w_u, w_d, w_pg, w_pp, g, b, *, tm, alpha):
    m, d = x.shape
    row = lambda i: (i, 0)
    return pl.pallas_call(
        functools.partial(_ffn_kernel, alpha=alpha),
        grid=(m // tm,),
        in_specs=[pl.BlockSpec((tm, d), row), pl.BlockSpec((tm, p.shape[1]), row)]
                 + [_const_spec(a.shape) for a in (w_g, w_u, w_d, w_pg, w_pp, g, b)],
        out_specs=pl.BlockSpec((tm, d), row),
        out_shape=jax.ShapeDtypeStruct((m, d), F32),
        compiler_params=_cparams(1),
        name="swiglu_ple_ln",
    )(x, p, w_g, w_u, w_d, w_pg, w_pp, g, b)


def _row_tile(m):
    for tm in (512, 384, 256, 128):
        if m % tm == 0:
            return tm
    raise ValueError(f"row count {m} is not a multiple of 128")


def kernel(x_prompt, x_sample, p_prompt, p_sample, cache_fox_k, cache_fox_v, cache_fox_logf, cache_diff_k, cache_diff_v, w_in, b_forget, lambda_q1, lambda_k1, lambda_q2, lambda_k2, diff_norm_g, w_branch_fox, w_branch_diff, w_out, ln1_g, ln1_b, w_ffn_gate, w_ffn_up, w_ffn_down, w_ple_gate, w_ple_proj, ln2_g, ln2_b):
    batch, seq, d_model = x_prompt.shape
    dec_batch, dec_seq, _ = x_sample.shape
    depth = w_in.shape[0]
    past = cache_fox_k.shape[2]
    h_a = cache_fox_k.shape[3]
    h_b = cache_diff_k.shape[3]
    w_a = h_a * HD_A
    w_b = h_b * 2 * DH_B
    assert w_a == w_b and w_in.shape[2] == 3 * w_a + h_a + 3 * w_b + 2 * d_model
    alpha = (2 * depth) ** 0.25
    m_p = batch * seq
    m_s = dec_batch * dec_seq
    m = m_p + m_s
    tm = _row_tile(m)
    tq = min(512, seq)
    tk_dec = min(1024, past)
    assert seq % tq == 0 and tq % CHUNK == 0 and past % tk_dec == 0 and m_p % dec_seq == 0
    t_dec = past + dec_seq
    t_dec_pad = -(-t_dec // SCAN_BLOCK) * SCAN_BLOCK

    x = jnp.concatenate([x_prompt.reshape(m_p, d_model), x_sample.reshape(m_s, d_model)], axis=0)

    o_f = 3 * w_a
    o_qb = o_f + h_a
    o_ga = o_qb + 3 * w_b

    new_p = [[] for _ in range(5)]
    new_s = [[] for _ in range(5)]
    for l in range(depth):
        lambda_init = 0.8 - 0.6 * math.exp(-0.3 * l)
        w = w_in[l]
        w_qkv = jnp.concatenate([w[:, :o_f], w[:, o_qb:o_ga]], axis=1).astype(BF16)
        w_f = jnp.pad(w[:, o_f:o_qb], ((0, 0), (0, LANES - h_a))).astype(BF16)
        b_f = jnp.pad(b_forget[l], (0, LANES - h_a)).reshape(1, LANES)
        w_gate = w[:, o_ga:].astype(BF16)

        (qa, qb, ka, va, kb, vb, ka_b, va_b, kb_b, vb_b, logf) = _inproj(
            x, w_qkv, w_f, b_f, tm=tm, n_heads=h_a)

        lf_p = logf[:m_p].reshape(batch, seq, h_a)
        lf_s = logf[m_p:].reshape(dec_batch, dec_seq, h_a)
        c_row_p = _cumsum_rows(jnp.swapaxes(lf_p, 1, 2).reshape(batch * h_a, seq))
        c_row_p = c_row_p.reshape(batch, h_a, seq)
        c_col_p = jnp.swapaxes(c_row_p, 1, 2)
        lf_all = jnp.concatenate([cache_fox_logf[l].astype(F32), lf_s], axis=1)
        lf_all = jnp.pad(lf_all, ((0, 0), (0, t_dec_pad - t_dec), (0, 0)))
        c_row_s = _cumsum_rows(jnp.swapaxes(lf_all, 1, 2).reshape(dec_batch * h_a, t_dec_pad))
        c_row_s = c_row_s.reshape(dec_batch, h_a, t_dec_pad)
        c_col_s = jnp.swapaxes(c_row_s, 1, 2)

        lams = [a[l].reshape(1, DH_B) for a in (lambda_q1, lambda_k1, lambda_q2, lambda_k2)]
        g_diff = diff_norm_g[l].reshape(1, w_b)

        oa_p = _fox_prompt(qa, ka_b, va_b, c_col_p, c_row_p,
                           batch=batch, seq=seq, tq=tq, n_heads=h_a)
        ob_p = _diff_prompt(qb, kb_b, vb_b, lams, g_diff,
                            batch=batch, seq=seq, tq=tq, n_heads=h_b, lambda_init=lambda_init)
        oa_s = _fox_decode(qa, cache_fox_k[l].reshape(dec_batch, past, w_a),
                           cache_fox_v[l].reshape(dec_batch, past, w_a), ka_b, va_b,
                           c_col_s, c_row_s, batch=dec_batch, tq=dec_seq, tk=tk_dec, row0=m_p,
                           n_heads=h_a)
        ob_s = _diff_decode(qb, cache_diff_k[l].reshape(dec_batch, past, w_b),
                            cache_diff_v[l].reshape(dec_batch, past, w_b), kb_b, vb_b,
                            lams, g_diff, batch=dec_batch, tq=dec_seq, tk=tk_dec, row0=m_p,
                            n_heads=h_b, lambda_init=lambda_init)
        oa = jnp.concatenate([oa_p, oa_s], axis=0)
        ob = jnp.concatenate([ob_p, ob_s], axis=0)

        x = _post(x, oa, ob, w_gate, w_branch_fox[l].astype(BF16), w_branch_diff[l].astype(BF16),
                  w_out[l].astype(BF16), ln1_g[l].reshape(1, d_model), ln1_b[l].reshape(1, d_model),
                  tm=tm, alpha=alpha)
        p = jnp.concatenate([p_prompt[l].reshape(m_p, -1), p_sample[l].reshape(m_s, -1)], axis=0)
        x = _ffn(x, p, w_ffn_gate[l].astype(BF16), w_ffn_up[l].astype(BF16),
                 w_ffn_down[l].astype(BF16), w_ple_gate[l].astype(BF16),
                 w_ple_proj[l].astype(BF16), ln2_g[l].reshape(1, d_model),
                 ln2_b[l].reshape(1, d_model), tm=tm, alpha=alpha)

        rows = (ka, va, logf, kb, vb)
        shapes_p = ((batch, seq, h_a, HD_A), (batch, seq, h_a, HD_A), (batch, seq, h_a),
                    (batch, seq, h_b, 2, DH_B), (batch, seq, h_b, 2 * DH_B))
        shapes_s = ((dec_batch, dec_seq, h_a, HD_A), (dec_batch, dec_seq, h_a, HD_A),
                    (dec_batch, dec_seq, h_a), (dec_batch, dec_seq, h_b, 2, DH_B),
                    (dec_batch, dec_seq, h_b, 2 * DH_B))
        for i, r in enumerate(rows):
            new_p[i].append(r[:m_p].reshape(shapes_p[i]))
            new_s[i].append(r[m_p:].reshape(shapes_s[i]))

    y_p = x[:m_p].reshape(batch, seq, d_model)
    y_s = x[m_p:].reshape(dec_batch, dec_seq, d_model)
    return (y_p, y_s, *[jnp.stack(a) for a in new_p], *[jnp.stack(a) for a in new_s])
```

```python
import functools
import math

import jax
import jax.numpy as jnp
from jax import lax
from jax.experimental import pallas as pl
from jax.experimental.pallas import tpu as pltpu

F32 = jnp.float32
BF16 = jnp.bfloat16

HD_A = 64
DH_B = 64
CHUNK = 64
CHUNK_SHIFT = 6
LN_EPS = 1e-5
RMS_EPS = 1e-5
NEG_INF = -1e30
LOG2E = math.log2(math.e)
LANES = 128
VMEM_LIMIT = 56 * 1024 * 1024


def _cparams(n_axes):
    return pltpu.CompilerParams(dimension_semantics=("arbitrary",) * n_axes,
                                vmem_limit_bytes=VMEM_LIMIT)


def _const_spec(shape):
    zeros = (0,) * len(shape)
    return pl.BlockSpec(shape, lambda *_: zeros, pipeline_mode=pl.Buffered(1))


def _dot(a, b):
    return jnp.dot(a, b, preferred_element_type=F32)


def _dot_nt(a, b):
    return lax.dot_general(a, b, (((1,), (1,)), ((), ())), preferred_element_type=F32)


def _sigmoid(x):
    return 1.0 / (1.0 + jnp.exp(-x))


def _log_sigmoid(z):
    return jnp.minimum(z, 0.0) - jnp.log1p(jnp.exp(-jnp.abs(z)))


def _layer_norm(y, g, b):
    mu = jnp.mean(y, axis=-1, keepdims=True)
    yc = y - mu
    var = jnp.mean(yc * yc, axis=-1, keepdims=True)
    return yc * lax.rsqrt(var + LN_EPS) * g + b


def _inproj_prompt_kernel(x_ref, wt_ref, wft_ref, bf_ref,
                          qat_ref, ka_ref, kat_ref, vat_ref, vatb_ref,
                          qbt_ref, kb_ref, kbt_ref, vb_ref, vbtb_ref, logf_ref,
                          *, width, n_heads, q_scale):
    xb = x_ref[...].astype(BF16)

    def proj_t(i):
        return _dot_nt(wt_ref[i * width:(i + 1) * width, :], xb)

    def proj(i):
        return _dot_nt(xb, wt_ref[i * width:(i + 1) * width, :])

    qat_ref[0] = (proj_t(0) * q_scale).astype(BF16)
    ka_ref[...] = proj(1).astype(BF16)
    kat_ref[0] = proj_t(1)
    vat = proj_t(2)
    vat_ref[0] = vat
    vatb_ref[0] = vat.astype(BF16)
    qbt_ref[0] = (proj_t(3) * q_scale).astype(BF16)
    kb_ref[...] = proj(4).astype(BF16)
    kbt_ref[0] = proj_t(4)
    vb = proj(5)
    for h in range(width // LANES):
        vb_ref[0, :, h, :] = vb[:, h * LANES:(h + 1) * LANES]
    vbtb_ref[0] = proj_t(5).astype(BF16)
    z = _dot_nt(wft_ref[...], xb) + bf_ref[...]
    logf_ref[0] = _log_sigmoid(z)[:n_heads, :]


def _inproj_prompt(x, wt_qkv, wt_f, b_f_col, *, batch, seq, tm, n_heads, q_scale):
    m, d = x.shape
    width = wt_qkv.shape[0] // 6
    nt = seq // tm
    row = lambda i: (i, 0)
    tr = lambda i: (i // nt, 0, i % nt)
    t_blk = pl.BlockSpec((1, width, tm), tr)
    n_blk = pl.BlockSpec((tm, width), row)
    t_f32 = jax.ShapeDtypeStruct((batch, width, seq), F32)
    t_b16 = jax.ShapeDtypeStruct((batch, width, seq), BF16)
    n_b16 = jax.ShapeDtypeStruct((m, width), BF16)
    vb_heads = width // LANES
    out_specs = [t_blk, n_blk, t_blk, t_blk, t_blk, t_blk, n_blk, t_blk,
                 pl.BlockSpec((1, tm, vb_heads, LANES), lambda i: (i // nt, i % nt, 0, 0)),
                 t_blk, pl.BlockSpec((1, n_heads, tm), tr)]
    out_shape = [t_b16, n_b16, t_f32, t_f32, t_b16, t_b16, n_b16, t_f32,
                 jax.ShapeDtypeStruct((batch, seq, vb_heads, LANES), F32),
                 t_b16, jax.ShapeDtypeStruct((batch, n_heads, seq), F32)]
    return pl.pallas_call(
        functools.partial(_inproj_prompt_kernel, width=width, n_heads=n_heads, q_scale=q_scale),
        grid=(m // tm,),
        in_specs=[pl.BlockSpec((tm, d), row), _const_spec(wt_qkv.shape), _const_spec(wt_f.shape),
                  _const_spec(b_f_col.shape)],
        out_specs=out_specs,
        out_shape=out_shape,
        compiler_params=_cparams(1),
        name="inproj_prompt",
    )(x, wt_qkv, wt_f, b_f_col)


def _inproj_decode_kernel(x_ref, wt_ref, wft_ref, bf_ref,
                          qa_ref, qb_ref, ka_ref, va_ref, kb_ref, vb_ref,
                          kab_ref, vab_ref, kbb_ref, vbb_ref, logf_ref, *, width, n_heads, q_scale):
    xb = x_ref[...].astype(BF16)

    def proj(i):
        return _dot_nt(xb, wt_ref[i * width:(i + 1) * width, :])

    qa_ref[...] = (proj(0) * q_scale).astype(BF16)
    qb_ref[...] = (proj(3) * q_scale).astype(BF16)
    for i, (full_ref, half_ref) in ((1, (ka_ref, kab_ref)), (2, (va_ref, vab_ref)),
                                    (4, (kb_ref, kbb_ref)), (5, (vb_ref, vbb_ref))):
        val = proj(i)
        full_ref[...] = val
        half_ref[...] = val.astype(BF16)
    z = _dot_nt(xb, wft_ref[...]) + bf_ref[...]
    logf_ref[...] = _log_sigmoid(z)[:, :n_heads]


def _inproj_decode(x, wt_qkv, wt_f, b_f_row, *, n_heads, q_scale):
    m, d = x.shape
    width = wt_qkv.shape[0] // 6
    out_shape = ([jax.ShapeDtypeStruct((m, width), BF16)] * 2
                 + [jax.ShapeDtypeStruct((m, width), F32)] * 4
                 + [jax.ShapeDtypeStruct((m, width), BF16)] * 4
                 + [jax.ShapeDtypeStruct((m, n_heads), F32)])
    return pl.pallas_call(
        functools.partial(_inproj_decode_kernel, width=width, n_heads=n_heads, q_scale=q_scale),
        out_shape=out_shape,
        compiler_params=pltpu.CompilerParams(vmem_limit_bytes=VMEM_LIMIT),
        name="inproj_decode",
    )(x, wt_qkv, wt_f, b_f_row)


SCAN_BLOCK = 256


def _split3(a):
    a1 = a.astype(BF16)
    r1 = a - a1.astype(F32)
    a2 = r1.astype(BF16)
    a3 = (r1 - a2.astype(F32)).astype(BF16)
    return a1, a2, a3


def _cumsum_kernel(lf_ref, c_ref, *, n_blk, scale):
    rows = lf_ref.shape[0]
    r = lax.broadcasted_iota(jnp.int32, (SCAN_BLOCK, SCAN_BLOCK), 0)
    c = lax.broadcasted_iota(jnp.int32, (SCAN_BLOCK, SCAN_BLOCK), 1)
    upper = jnp.where(r <= c, 1.0, 0.0).astype(BF16)
    carry = jnp.zeros((rows, 1), F32)
    for j in range(n_blk):
        sl = slice(j * SCAN_BLOCK, (j + 1) * SCAN_BLOCK)
        a1, a2, a3 = _split3(lf_ref[:, sl])
        blk = _dot(a1, upper) + _dot(a2, upper) + _dot(a3, upper) + carry
        c_ref[:, sl] = blk * scale
        carry = blk[:, SCAN_BLOCK - 1:SCAN_BLOCK]


def _cumsum_rows(lf_rows, scale):
    rows, t = lf_rows.shape
    assert t % SCAN_BLOCK == 0
    return pl.pallas_call(
        functools.partial(_cumsum_kernel, n_blk=t // SCAN_BLOCK, scale=scale),
        out_shape=jax.ShapeDtypeStruct((rows, t), F32),
        compiler_params=pltpu.CompilerParams(vmem_limit_bytes=VMEM_LIMIT),
        name="logf_cumsum",
    )(lf_rows)


def _lambda_scalar(lq1, lk1, lq2, lk2, lambda_init):
    return (jnp.exp(jnp.sum(lq1 * lk1, axis=1, keepdims=True))
            - jnp.exp(jnp.sum(lq2 * lk2, axis=1, keepdims=True)) + lambda_init)


def _head_rms(o, g_row, lambda_init):
    o = o * lax.rsqrt(jnp.mean(o * o, axis=-1, keepdims=True) + RMS_EPS)
    return o * g_row * (1.0 - lambda_init)


def _alibi_slope(h, n_heads):
    return 2.0 ** (-8.0 * (h + 1) / n_heads)


def _sublane_half_masks(cols):
    row = lax.broadcasted_iota(jnp.int32, (LANES, cols), 0)
    lo = jnp.where(row < HD_A, 1.0, 0.0)
    return lo, 1.0 - lo


def _flash_step_t(t, shift, v_t, carry):
    m, l, acc = carry
    m_new = jnp.maximum(m, jnp.max(t, axis=0, keepdims=True) + shift)
    alpha = jnp.exp2(m - m_new)
    p = jnp.exp2(t - (m_new - shift))
    l = alpha * l + jnp.sum(p, axis=0, keepdims=True)
    acc = alpha * acc + _dot(v_t, p.astype(BF16))
    return m_new, l, acc


def _flash_init_t(d, tq):
    return (jnp.full((1, tq), NEG_INF, F32), jnp.zeros((1, tq), F32), jnp.zeros((d, tq), F32))


def _flash_step_refs(i, t, shift, v_t, m_ref, l_ref, acc_ref):
    m_ref[i], l_ref[i], acc_ref[i] = _flash_step_t(t, shift, v_t, (m_ref[i], l_ref[i], acc_ref[i]))


def _flash_init_refs(m_ref, l_ref, acc_ref):
    m_ref[...] = jnp.full(m_ref.shape, NEG_INF, F32)
    l_ref[...] = jnp.zeros(l_ref.shape, F32)
    acc_ref[...] = jnp.zeros(acc_ref.shape, F32)


def _fox_prompt_kernel(qt_ref, k_ref, vt_ref, crow_ref, ccol_ref, o_ref,
                       qm_ref, m_ref, l_ref, acc_ref, *, tq, tk, n_heads):
    qi = pl.program_id(1)
    n_diag = tq // tk
    lo, hi = _sublane_half_masks(tq)
    for h in range(n_heads):
        blk = slice((h // 2) * LANES, (h // 2 + 1) * LANES)
        qm_ref[h] = (qt_ref[0, blk, :].astype(F32) * (lo, hi)[h % 2]).astype(BF16)
    _flash_init_refs(m_ref, l_ref, acc_ref)

    def tile(ks, mask):
        ck = ccol_ref[0, pl.ds(ks, tk), :]
        v_all = vt_ref[0, :, pl.ds(ks, tk)]
        scores = []
        for h in range(n_heads):
            blk = slice((h // 2) * LANES, (h // 2 + 1) * LANES)
            t = _dot(k_ref[pl.ds(ks, tk), blk], qm_ref[h]) - ck[:, h:h + 1]
            if mask is not None:
                t = jnp.where(mask, t, NEG_INF)
            scores.append(t)
        for h in range(n_heads):
            _flash_step_refs(h, scores[h], crow_ref[0, h:h + 1, :], v_all[h * HD_A:(h + 1) * HD_A],
                             m_ref, l_ref, acc_ref)

    def full_tile(j, carry):
        tile(pl.multiple_of(j * tk, tk), None)
        return carry

    lax.fori_loop(0, qi * n_diag, full_tile, 0)
    r = lax.broadcasted_iota(jnp.int32, (tk, tq), 0)
    c = lax.broadcasted_iota(jnp.int32, (tk, tq), 1)
    for d in range(n_diag):
        tile(pl.multiple_of(qi * tq + d * tk, tk), (r + d * tk) <= c)
    for pair in range(n_heads // 2):
        blk = slice(pair * LANES, (pair + 1) * LANES)
        o_t = jnp.concatenate([acc_ref[2 * pair] / l_ref[2 * pair],
                               acc_ref[2 * pair + 1] / l_ref[2 * pair + 1]], axis=0)
        o_ref[:, blk] = o_t.T.astype(BF16)


def _fox_prompt(qt, k, vt, c_row, c_col, *, batch, seq, tq, tk, n_heads):
    width = n_heads * HD_A
    nq = seq // tq
    return pl.pallas_call(
        functools.partial(_fox_prompt_kernel, tq=tq, tk=tk, n_heads=n_heads),
        grid=(batch, nq),
        in_specs=[pl.BlockSpec((1, width, tq), lambda b, i: (b, 0, i)),
                  pl.BlockSpec((seq, width), lambda b, i: (b, 0)),
                  pl.BlockSpec((1, width, seq), lambda b, i: (b, 0, 0)),
                  pl.BlockSpec((1, n_heads, tq), lambda b, i: (b, 0, i)),
                  pl.BlockSpec((1, seq, n_heads), lambda b, i: (b, 0, 0))],
        out_specs=pl.BlockSpec((tq, width), lambda b, i: (b * nq + i, 0)),
        out_shape=jax.ShapeDtypeStruct((batch * seq, width), BF16),
        scratch_shapes=[pltpu.VMEM((n_heads, LANES, tq), BF16), pltpu.VMEM((n_heads, 1, tq), F32),
                        pltpu.VMEM((n_heads, 1, tq), F32), pltpu.VMEM((n_heads, HD_A, tq), F32)],
        compiler_params=_cparams(2),
        name="fox_prompt",
    )(qt, k, vt, c_row, c_col)


def _diff_prompt_kernel(qt_ref, k_ref, vt_ref, lq1_ref, lk1_ref, lq2_ref, lk2_ref, g_ref, o_ref,
                        qm_ref, m_ref, l_ref, acc_ref, *, tq, tk, n_heads, lambda_init):
    qi = pl.program_id(1)
    n_diag = tq // tk
    lo, hi = _sublane_half_masks(tq)
    for i in range(2 * n_heads):
        blk = slice((i // 2) * LANES, (i // 2 + 1) * LANES)
        qm_ref[i] = (qt_ref[0, blk, :].astype(F32) * (lo, hi)[i % 2]).astype(BF16)
    _flash_init_refs(m_ref, l_ref, acc_ref)
    r = lax.broadcasted_iota(jnp.int32, (tk, tq), 0)
    c = lax.broadcasted_iota(jnp.int32, (tk, tq), 1)
    rel = (r - c).astype(F32)

    def tile(ks, bias_of, shift_of, mask):
        for h in range(n_heads):
            blk = slice(h * LANES, (h + 1) * LANES)
            slope2 = _alibi_slope(h, n_heads) * LOG2E
            k2 = k_ref[pl.ds(ks, tk), blk]
            v_t = vt_ref[0, blk, pl.ds(ks, tk)]
            bias = bias_of(slope2)
            for comp in range(2):
                t = _dot(k2, qm_ref[2 * h + comp]) + bias
                if mask is not None:
                    t = jnp.where(mask, t, NEG_INF)
                _flash_step_refs(2 * h + comp, t, shift_of(slope2), v_t, m_ref, l_ref, acc_ref)

    def full_tile(j, carry):
        off = (j * tk - qi * tq).astype(F32)
        tile(pl.multiple_of(j * tk, tk), lambda s2: s2 * rel, lambda s2: s2 * off, None)
        return carry

    lax.fori_loop(0, qi * n_diag, full_tile, 0)
    q_chunk = lax.shift_right_logical(c, CHUNK_SHIFT)
    for d in range(n_diag):
        dist = jnp.abs(rel + float(d * tk))
        visible = lax.shift_right_logical(r + d * tk, CHUNK_SHIFT) <= q_chunk
        tile(pl.multiple_of(qi * tq + d * tk, tk), lambda s2: -s2 * dist, lambda s2: 0.0, visible)
    lam = _lambda_scalar(lq1_ref[...], lk1_ref[...], lq2_ref[...], lk2_ref[...], lambda_init)
    for h in range(n_heads):
        blk = slice(h * LANES, (h + 1) * LANES)
        o = (acc_ref[2 * h] / l_ref[2 * h] - lam * (acc_ref[2 * h + 1] / l_ref[2 * h + 1])).T
        o_ref[:, blk] = _head_rms(o, g_ref[:, blk], lambda_init).astype(BF16)


def _diff_prompt(qt, k, vt, lams, g, *, batch, seq, tq, tk, n_heads, lambda_init):
    width = n_heads * 2 * DH_B
    nq = seq // tq
    return pl.pallas_call(
        functools.partial(_diff_prompt_kernel, tq=tq, tk=tk, n_heads=n_heads,
                          lambda_init=lambda_init),
        grid=(batch, nq),
        in_specs=[pl.BlockSpec((1, width, tq), lambda b, i: (b, 0, i)),
                  pl.BlockSpec((seq, width), lambda b, i: (b, 0)),
                  pl.BlockSpec((1, width, seq), lambda b, i: (b, 0, 0))]
                 + [_const_spec(a.shape) for a in lams] + [_const_spec(g.shape)],
        out_specs=pl.BlockSpec((tq, width), lambda b, i: (b * nq + i, 0)),
        out_shape=jax.ShapeDtypeStruct((batch * seq, width), BF16),
        scratch_shapes=[pltpu.VMEM((2 * n_heads, LANES, tq), BF16),
                        pltpu.VMEM((2 * n_heads, 1, tq), F32),
                        pltpu.VMEM((2 * n_heads, 1, tq), F32),
                        pltpu.VMEM((2 * n_heads, LANES, tq), F32)],
        compiler_params=_cparams(2),
        name="diff_prompt",
    )(qt, k, vt, *lams, g)


def _lane_half_masks(rows):
    lane = lax.broadcasted_iota(jnp.int32, (rows, LANES), 1)
    lo = jnp.where(lane < HD_A, 1.0, 0.0)
    return lo, 1.0 - lo


def _masked_q(q2, mask):
    return (q2.astype(F32) * mask).astype(BF16)


def _flash_init(m_ref, l_ref, acc_ref):
    m_ref[...] = jnp.full(m_ref.shape, NEG_INF, F32)
    l_ref[...] = jnp.zeros(l_ref.shape, F32)
    acc_ref[...] = jnp.zeros(acc_ref.shape, F32)


def _flash_step(s, pv, m_ref, l_ref, acc_ref):
    m_prev = m_ref[...]
    m_new = jnp.maximum(m_prev, jnp.max(s, axis=1, keepdims=True))
    alpha = jnp.exp(m_prev - m_new)
    p = jnp.exp(s - m_new)
    l_ref[...] = alpha * l_ref[...] + jnp.sum(p, axis=1, keepdims=True)
    acc_ref[...] = alpha * acc_ref[...] + pv(p.astype(BF16))
    m_ref[...] = m_new


def _fox_decode_kernel(q_ref, kc_ref, vc_ref, kn_ref, vn_ref, ccol_ref, crow_ref, o_ref,
                       m_ref, l_ref, acc_ref, *, tq, tk, n_kv, n_heads):
    j = pl.program_id(1)
    lo, hi = _lane_half_masks(tq)

    @pl.when(j == 0)
    def _():
        _flash_init(m_ref, l_ref, acc_ref)

    def head_loop(scores_of, pv_of, bias_of, mask):
        for pair in range(n_heads // 2):
            blk = slice(pair * LANES, (pair + 1) * LANES)
            q2 = q_ref[:, blk]
            scores = scores_of(blk)
            pv = pv_of(blk)
            for half, hm in enumerate((lo, hi)):
                h = 2 * pair + half
                s = scores(_masked_q(q2, hm)) + bias_of(h)
                if mask is not None:
                    s = jnp.where(mask, s, NEG_INF)
                _flash_step(s, pv, m_ref.at[h], l_ref.at[h], acc_ref.at[h])

    @pl.when(j < n_kv)
    def _():
        ks = pl.multiple_of(j * tk, tk)

        def scores_of(blk):
            kt = kc_ref[0, 0, blk, :].astype(BF16)
            return lambda qm: _dot(qm, kt)

        def pv_of(blk):
            vt = vc_ref[0, 0, blk, :].astype(BF16)
            return lambda p: _dot_nt(p, vt)

        head_loop(scores_of, pv_of,
                  lambda h: ccol_ref[0, :, h:h + 1] - crow_ref[0, h:h + 1, pl.ds(ks, tk)], None)

    @pl.when(j == n_kv)
    def _():
        past = n_kv * tk
        q_pos = lax.broadcasted_iota(jnp.int32, (tq, tq), 0)
        k_pos = lax.broadcasted_iota(jnp.int32, (tq, tq), 1)
        head_loop(lambda blk: (lambda qm: _dot_nt(qm, kn_ref[:, blk])),
                  lambda blk: (lambda p: _dot(p, vn_ref[:, blk])),
                  lambda h: ccol_ref[0, :, h:h + 1] - crow_ref[0, h:h + 1, past:past + tq],
                  k_pos <= q_pos)
        for pair in range(n_heads // 2):
            blk = slice(pair * LANES, (pair + 1) * LANES)
            o0 = acc_ref[2 * pair] / l_ref[2 * pair]
            o1 = acc_ref[2 * pair + 1] / l_ref[2 * pair + 1]
            o_ref[:, blk] = (o0 * lo + o1 * hi).astype(BF16)


def _fox_decode(q, kt_cache, vt_cache, k_new, v_new, c_col, c_row, *, layer, tq, tk, n_heads):
    _, batch, width, past = kt_cache.shape
    n_kv = past // tk
    new_blk = pl.BlockSpec((tq, width), lambda b, j: (b, 0))
    cache_blk = pl.BlockSpec((1, 1, width, tk),
                             lambda b, j: (layer, b, 0, jnp.minimum(j, n_kv - 1)))
    return pl.pallas_call(
        functools.partial(_fox_decode_kernel, tq=tq, tk=tk, n_kv=n_kv, n_heads=n_heads),
        grid=(batch, n_kv + 1),
        in_specs=[new_blk, cache_blk, cache_blk, new_blk, new_blk,
                  pl.BlockSpec((1, tq, n_heads), lambda b, j: (b, past // tq, 0)),
                  pl.BlockSpec((1, n_heads, c_row.shape[2]), lambda b, j: (b, 0, 0))],
        out_specs=new_blk,
        out_shape=jax.ShapeDtypeStruct((batch * tq, width), BF16),
        scratch_shapes=[pltpu.VMEM((n_heads, tq, 1), F32), pltpu.VMEM((n_heads, tq, 1), F32),
                        pltpu.VMEM((n_heads, tq, LANES), F32)],
        compiler_params=_cparams(2),
        name="fox_decode",
    )(q, kt_cache, vt_cache, k_new, v_new, c_col, c_row)


def _diff_decode_kernel(q_ref, kc_ref, vc_ref, kn_ref, vn_ref, lq1_ref, lk1_ref, lq2_ref, lk2_ref,
                        g_ref, o_ref, m_ref, l_ref, acc_ref,
                        *, tq, tk, n_kv, n_heads, lambda_init):
    j = pl.program_id(1)
    lo, hi = _lane_half_masks(tq)
    past = n_kv * tk
    q_abs = (past + lax.broadcasted_iota(jnp.int32, (tq, 1), 0)).astype(F32)

    @pl.when(j == 0)
    def _():
        _flash_init(m_ref, l_ref, acc_ref)

    def head_loop(scores_of, pv_of, dist):
        for h in range(n_heads):
            blk = slice(h * LANES, (h + 1) * LANES)
            slope = _alibi_slope(h, n_heads)
            q2 = q_ref[:, blk]
            scores = scores_of(blk)
            pv = pv_of(h)
            for comp, hm in enumerate((lo, hi)):
                s = scores(_masked_q(q2, hm)) - slope * dist
                idx = 2 * h + comp
                _flash_step(s, pv, m_ref.at[idx], l_ref.at[idx], acc_ref.at[idx])

    @pl.when(j < n_kv)
    def _():
        k_abs = (j * tk + lax.broadcasted_iota(jnp.int32, (1, tk), 1)).astype(F32)

        def scores_of(blk):
            kt = kc_ref[0, 0, blk, :].astype(BF16)
            return lambda qm: _dot(qm, kt)

        def pv_of(h):
            v2 = vc_ref[0, 0, :, h, :].astype(BF16)
            return lambda p: _dot(p, v2)

        head_loop(scores_of, pv_of, jnp.abs(q_abs - k_abs))

    @pl.when(j == n_kv)
    def _():
        assert past % CHUNK == 0 and tq <= CHUNK
        k_abs = (past + lax.broadcasted_iota(jnp.int32, (1, tq), 1)).astype(F32)
        head_loop(lambda blk: (lambda qm: _dot_nt(qm, kn_ref[:, blk])),
                  lambda h: (lambda p: _dot(p, vn_ref[:, h * LANES:(h + 1) * LANES])),
                  jnp.abs(q_abs - k_abs))
        lam = _lambda_scalar(lq1_ref[...], lk1_ref[...], lq2_ref[...], lk2_ref[...], lambda_init)
        for h in range(n_heads):
            blk = slice(h * LANES, (h + 1) * LANES)
            o = acc_ref[2 * h] / l_ref[2 * h] - lam * (acc_ref[2 * h + 1] / l_ref[2 * h + 1])
            o_ref[:, blk] = _head_rms(o, g_ref[:, blk], lambda_init).astype(BF16)


def _diff_decode(q, kt_cache, v_cache, k_new, v_new, lams, g, *, layer, tq, tk, n_heads,
                 lambda_init):
    _, batch, width, past = kt_cache.shape
    n_kv = past // tk
    new_blk = pl.BlockSpec((tq, width), lambda b, j: (b, 0))
    return pl.pallas_call(
        functools.partial(_diff_decode_kernel, tq=tq, tk=tk, n_kv=n_kv, n_heads=n_heads,
                          lambda_init=lambda_init),
        grid=(batch, n_kv + 1),
        in_specs=[new_blk,
                  pl.BlockSpec((1, 1, width, tk),
                               lambda b, j: (layer, b, 0, jnp.minimum(j, n_kv - 1))),
                  pl.BlockSpec((1, 1, tk, n_heads, LANES),
                               lambda b, j: (layer, b, jnp.minimum(j, n_kv - 1), 0, 0)),
                  new_blk, new_blk]
                 + [_const_spec(a.shape) for a in lams] + [_const_spec(g.shape)],
        out_specs=new_blk,
        out_shape=jax.ShapeDtypeStruct((batch * tq, width), BF16),
        scratch_shapes=[pltpu.VMEM((2 * n_heads, tq, 1), F32),
                        pltpu.VMEM((2 * n_heads, tq, 1), F32),
                        pltpu.VMEM((2 * n_heads, tq, LANES), F32)],
        compiler_params=_cparams(2),
        name="diff_decode",
    )(q, kt_cache, v_cache, k_new, v_new, *lams, g)


def _post_kernel(x_ref, oa_ref, ob_ref, wgt_ref, wba_ref, wbb_ref, wo_ref, g_ref, b_ref, o_ref,
                 *, alpha):
    x = x_ref[...]
    xb = x.astype(BF16)
    d = x.shape[1]
    merged = _sigmoid(_dot_nt(xb, wgt_ref[:d, :])) * _dot(oa_ref[...], wba_ref[...])
    merged = merged + _sigmoid(_dot_nt(xb, wgt_ref[d:, :])) * _dot(ob_ref[...], wbb_ref[...])
    y = alpha * x + _dot(merged.astype(BF16), wo_ref[...])
    o_ref[...] = _layer_norm(y, g_ref[...], b_ref[...])


def _post(x, oa, ob, wt_gate, w_ba, w_bb, w_o, g, b, *, tm, alpha):
    m, d = x.shape
    row = lambda i: (i, 0)
    return pl.pallas_call(
        functools.partial(_post_kernel, alpha=alpha),
        grid=(m // tm,),
        in_specs=[pl.BlockSpec((tm, d), row), pl.BlockSpec((tm, oa.shape[1]), row),
                  pl.BlockSpec((tm, ob.shape[1]), row)]
                 + [_const_spec(a.shape) for a in (wt_gate, w_ba, w_bb, w_o, g, b)],
        out_specs=pl.BlockSpec((tm, d), row),
        out_shape=jax.ShapeDtypeStruct((m, d), F32),
        compiler_params=_cparams(1),
        name="merge_outproj_ln",
    )(x, oa, ob, wt_gate, w_ba, w_bb, w_o, g, b)


def _ffn_kernel(x_ref, p_ref, wg_ref, wu_ref, wd_ref, wpg_ref, wpp_ref, g_ref, b_ref, o_ref,
                *, alpha):
    x = x_ref[...]
    xb = x.astype(BF16)
    hg = _dot(xb, wg_ref[...])
    hidden = (hg * _sigmoid(hg) * _dot(xb, wu_ref[...])).astype(BF16)
    y = alpha * x + _dot(hidden, wd_ref[...])
    ple = _sigmoid(_dot(xb, wpg_ref[...])) * _dot(p_ref[...].astype(BF16), wpp_ref[...])
    o_ref[...] = _layer_norm(y + ple, g_ref[...], b_ref[...])


def _ffn(x, p, w_g, w_u, w_d, w_pg, w_pp, g, b, *, tm, alpha):
    m, d = x.shape
    row = lambda i: (i, 0)
    return pl.pallas_call(
        functools.partial(_ffn_kernel, alpha=alpha),
        grid=(m // tm,),
        in_specs=[pl.BlockSpec((tm, d), row), pl.BlockSpec((tm, p.shape[1]), row)]
                 + [_const_spec(a.shape) for a in (w_g, w_u, w_d, w_pg, w_pp, g, b)],
        out_specs=pl.BlockSpec((tm, d), row),
        out_shape=jax.ShapeDtypeStruct((m, d), F32),
        compiler_params=_cparams(1),
        name="swiglu_ple_ln",
    )(x, p, w_g, w_u, w_d, w_pg, w_pp, g, b)


def _row_tile(m):
    for tm in (512, 256, 128):
        if m % tm == 0:
            return tm
    raise ValueError(f"row count {m} is not a multiple of 128")


def kernel(x_prompt, x_sample, p_prompt, p_sample, cache_fox_k, cache_fox_v, cache_fox_logf, cache_diff_k, cache_diff_v, w_in, b_forget, lambda_q1, lambda_k1, lambda_q2, lambda_k2, diff_norm_g, w_branch_fox, w_branch_diff, w_out, ln1_g, ln1_b, w_ffn_gate, w_ffn_up, w_ffn_down, w_ple_gate, w_ple_proj, ln2_g, ln2_b):
    batch, seq, d_model = x_prompt.shape
    dec_batch, dec_seq, _ = x_sample.shape
    depth = w_in.shape[0]
    past = cache_fox_k.shape[2]
    h_a = cache_fox_k.shape[3]
    h_b = cache_diff_k.shape[3]
    w_a = h_a * HD_A
    w_b = h_b * 2 * DH_B
    assert w_a == w_b and w_in.shape[2] == 3 * w_a + h_a + 3 * w_b + 2 * d_model
    assert HD_A == DH_B
    alpha = (2 * depth) ** 0.25
    m_p = batch * seq
    m_s = dec_batch * dec_seq
    tm_p = _row_tile(seq)
    tm_s = _row_tile(m_s)
    tq, tk = min(256, seq), min(128, seq)
    tk_dec = min(1024, past)
    assert seq % tq == 0 and tq % tk == 0 and tk % CHUNK == 0 and past % tk_dec == 0
    t_dec = past + dec_seq
    t_dec_pad = -(-t_dec // SCAN_BLOCK) * SCAN_BLOCK
    qk_scale = HD_A ** -0.5
    f_pad = 16

    xp = x_prompt.reshape(m_p, d_model)
    xs = x_sample.reshape(m_s, d_model)

    wt_in = jnp.swapaxes(w_in, 1, 2)
    kt_cache_a = jnp.transpose(cache_fox_k, (0, 1, 3, 4, 2)).reshape(depth, dec_batch, w_a, past)
    vt_cache_a = jnp.transpose(cache_fox_v, (0, 1, 3, 4, 2)).reshape(depth, dec_batch, w_a, past)
    kt_cache_b = jnp.transpose(cache_diff_k, (0, 1, 3, 4, 5, 2)).reshape(depth, dec_batch, w_b, past)
    lf_cache = jnp.swapaxes(cache_fox_logf, 2, 3).astype(F32)

    o_f = 3 * w_a
    o_qb = o_f + h_a
    o_ga = o_qb + 3 * w_b

    new_p = [[] for _ in range(5)]
    new_s = [[] for _ in range(5)]
    for l in range(depth):
        lambda_init = 0.8 - 0.6 * math.exp(-0.3 * l)
        wt = wt_in[l]
        wt_qkv = jnp.concatenate([wt[:o_f], wt[o_qb:o_ga]], axis=0).astype(BF16)
        wt_f = jnp.pad(wt[o_f:o_qb], ((0, f_pad - h_a), (0, 0))).astype(BF16)
        b_f = jnp.pad(b_forget[l], (0, f_pad - h_a))
        wt_gate = wt[o_ga:].astype(BF16)

        (qat, ka, kat, vat, vat_b, qbt, kb, kbt, vb, vbt_b, logf_p) = _inproj_prompt(
            xp, wt_qkv, wt_f, b_f.reshape(f_pad, 1), batch=batch, seq=seq, tm=tm_p, n_heads=h_a,
            q_scale=qk_scale * LOG2E)
        (qa_s, qb_s, ka_s, va_s, kb_s, vb_s, ka_sb, va_sb, kb_sb, vb_sb, logf_s) = _inproj_decode(
            xs, wt_qkv, wt_f, b_f.reshape(1, f_pad), n_heads=h_a, q_scale=qk_scale)

        c_row_p = _cumsum_rows(logf_p.reshape(batch * h_a, seq), LOG2E).reshape(batch, h_a, seq)
        c_col_p = jnp.swapaxes(c_row_p, 1, 2)
        lf_s = jnp.swapaxes(logf_s.reshape(dec_batch, dec_seq, h_a), 1, 2)
        lf_all = jnp.concatenate([lf_cache[l], lf_s], axis=2)
        lf_all = jnp.pad(lf_all, ((0, 0), (0, 0), (0, t_dec_pad - t_dec)))
        c_row_s = _cumsum_rows(lf_all.reshape(dec_batch * h_a, t_dec_pad), 1.0)
        c_row_s = c_row_s.reshape(dec_batch, h_a, t_dec_pad)
        c_col_s = jnp.swapaxes(c_row_s, 1, 2)

        lams = [a[l].reshape(1, DH_B) for a in (lambda_q1, lambda_k1, lambda_q2, lambda_k2)]
        g_diff = diff_norm_g[l].reshape(1, w_b)

        oa_p = _fox_prompt(qat, ka, vat_b, c_row_p, c_col_p,
                           batch=batch, seq=seq, tq=tq, tk=tk, n_heads=h_a)
        ob_p = _diff_prompt(qbt, kb, vbt_b, lams, g_diff, batch=batch, seq=seq, tq=tq, tk=tk,
                            n_heads=h_b, lambda_init=lambda_init)
        oa_s = _fox_decode(qa_s, kt_cache_a, vt_cache_a, ka_sb, va_sb, c_col_s, c_row_s,
                           layer=l, tq=dec_seq, tk=tk_dec, n_heads=h_a)
        ob_s = _diff_decode(qb_s, kt_cache_b, cache_diff_v, kb_sb, vb_sb, lams, g_diff,
                            layer=l, tq=dec_seq, tk=tk_dec, n_heads=h_b, lambda_init=lambda_init)

        dense_w = (wt_gate, w_branch_fox[l].astype(BF16), w_branch_diff[l].astype(BF16),
                   w_out[l].astype(BF16), ln1_g[l].reshape(1, d_model), ln1_b[l].reshape(1, d_model))
        ffn_w = (w_ffn_gate[l].astype(BF16), w_ffn_up[l].astype(BF16), w_ffn_down[l].astype(BF16),
                 w_ple_gate[l].astype(BF16), w_ple_proj[l].astype(BF16),
                 ln2_g[l].reshape(1, d_model), ln2_b[l].reshape(1, d_model))
        xp = _post(xp, oa_p, ob_p, *dense_w, tm=tm_p, alpha=alpha)
        xp = _ffn(xp, p_prompt[l].reshape(m_p, -1), *ffn_w, tm=tm_p, alpha=alpha)
        xs = _post(xs, oa_s, ob_s, *dense_w, tm=tm_s, alpha=alpha)
        xs = _ffn(xs, p_sample[l].reshape(m_s, -1), *ffn_w, tm=tm_s, alpha=alpha)

        for lst, r in zip(new_p, (kat, vat, logf_p, kbt, vb)):
            lst.append(r)
        for lst, r in zip(new_s, (ka_s, va_s, logf_s, kb_s, vb_s)):
            lst.append(r)

    kat, vat, logf_p, kbt, vb = [jnp.stack(a) for a in new_p]
    ka_s, va_s, logf_s, kb_s, vb_s = [jnp.stack(a) for a in new_s]
    fox_k_p = jnp.transpose(kat.reshape(depth, batch, h_a, HD_A, seq), (0, 1, 4, 2, 3))
    fox_v_p = jnp.transpose(vat.reshape(depth, batch, h_a, HD_A, seq), (0, 1, 4, 2, 3))
    fox_lf_p = jnp.swapaxes(logf_p, 2, 3)
    diff_k_p = jnp.transpose(kbt.reshape(depth, batch, h_b, 2, DH_B, seq), (0, 1, 5, 2, 3, 4))
    return (xp.reshape(batch, seq, d_model), xs.reshape(dec_batch, dec_seq, d_model),
            fox_k_p, fox_v_p, fox_lf_p, diff_k_p, vb,
            ka_s.reshape(depth, dec_batch, dec_seq, h_a, HD_A),
            va_s.reshape(depth, dec_batch, dec_seq, h_a, HD_A),
            logf_s.reshape(depth, dec_batch, dec_seq, h_a),
            kb_s.reshape(depth, dec_batch, dec_seq, h_b, 2, DH_B),
            vb_s.reshape(depth, dec_batch, dec_seq, h_b, 2 * DH_B))
```

```python
import functools
import math

import jax
import jax.numpy as jnp
from jax import lax
from jax.experimental import pallas as pl
from jax.experimental.pallas import tpu as pltpu

F32 = jnp.float32
BF16 = jnp.bfloat16

HD_A = 64
DH_B = 64
CHUNK = 64
CHUNK_SHIFT = 6
LN_EPS = 1e-5
RMS_EPS = 1e-5
NEG_INF = -1e30
LOG2E = math.log2(math.e)
LANES = 128
VMEM_LIMIT = 56 * 1024 * 1024


def _cparams(n_axes):
    return pltpu.CompilerParams(dimension_semantics=("arbitrary",) * n_axes,
                                vmem_limit_bytes=VMEM_LIMIT)


def _const_spec(shape):
    zeros = (0,) * len(shape)
    return pl.BlockSpec(shape, lambda *_: zeros, pipeline_mode=pl.Buffered(1))


def _dot(a, b):
    return jnp.dot(a, b, preferred_element_type=F32)


def _dot_nt(a, b):
    return lax.dot_general(a, b, (((1,), (1,)), ((), ())), preferred_element_type=F32)


def _sigmoid(x):
    return 1.0 / (1.0 + jnp.exp(-x))


def _log_sigmoid(z):
    return jnp.minimum(z, 0.0) - jnp.log1p(jnp.exp(-jnp.abs(z)))


def _layer_norm(y, g, b):
    mu = jnp.mean(y, axis=-1, keepdims=True)
    yc = y - mu
    var = jnp.mean(yc * yc, axis=-1, keepdims=True)
    return yc * lax.rsqrt(var + LN_EPS) * g + b


def _inproj_prompt_kernel(x_ref, wt_ref, wft_ref, bf_ref, *refs, width, n_heads, q_scale):
    (qat_ref, ka_ref, kat_ref, vat_ref, vatb_ref,
     qbt_ref, kb_ref, kbt_ref, vb_ref, vbtb_ref, logf_ref) = refs[-11:]
    xb = x_ref[...].astype(BF16)

    def proj_t(i):
        return _dot_nt(wt_ref[i * width:(i + 1) * width, :], xb)

    qat_ref[0] = (proj_t(0) * q_scale).astype(BF16)
    kat = proj_t(1)
    kat_ref[0, 0] = kat
    ka_ref[...] = kat.T.astype(BF16)
    vat = proj_t(2)
    vat_ref[0, 0] = vat
    vatb_ref[0] = vat.astype(BF16)
    qbt_ref[0] = (proj_t(3) * q_scale).astype(BF16)
    kbt = proj_t(4)
    kbt_ref[0, 0] = kbt
    kb_ref[...] = kbt.T.astype(BF16)
    vbt = proj_t(5)
    vbtb_ref[0] = vbt.astype(BF16)
    vb = vbt.T
    for h in range(width // LANES):
        vb_ref[0, 0, :, h, :] = vb[:, h * LANES:(h + 1) * LANES]
    z = _dot_nt(wft_ref[...], xb) + bf_ref[...]
    logf_ref[0, 0] = _log_sigmoid(z)[:n_heads, :]


def _inproj_prompt(x, wt_qkv, wt_f, b_f_col, stacked, *, layer, depth, batch, seq, tm, n_heads,
                   q_scale):
    m, d = x.shape
    width = wt_qkv.shape[0] // 6
    nt = seq // tm
    row = lambda i: (i, 0)
    tr = lambda i: (i // nt, 0, i % nt)
    tr_l = lambda i: (layer, i // nt, 0, i % nt)
    t_blk = pl.BlockSpec((1, width, tm), tr)
    t_blk_l = pl.BlockSpec((1, 1, width, tm), tr_l)
    n_blk = pl.BlockSpec((tm, width), row)
    t_f32 = jax.ShapeDtypeStruct((depth, batch, width, seq), F32)
    t_b16 = jax.ShapeDtypeStruct((batch, width, seq), BF16)
    n_b16 = jax.ShapeDtypeStruct((m, width), BF16)
    vb_heads = width // LANES
    out_specs = [t_blk, n_blk, t_blk_l, t_blk_l, t_blk, t_blk, n_blk, t_blk_l,
                 pl.BlockSpec((1, 1, tm, vb_heads, LANES),
                              lambda i: (layer, i // nt, i % nt, 0, 0)),
                 t_blk, pl.BlockSpec((1, 1, n_heads, tm), tr_l)]
    out_shape = [t_b16, n_b16, t_f32, t_f32, t_b16, t_b16, n_b16, t_f32,
                 jax.ShapeDtypeStruct((depth, batch, seq, vb_heads, LANES), F32),
                 t_b16, jax.ShapeDtypeStruct((depth, batch, n_heads, seq), F32)]
    stacked_outputs = (2, 3, 7, 8, 10)
    in_specs = [pl.BlockSpec((tm, d), row), _const_spec(wt_qkv.shape), _const_spec(wt_f.shape),
                _const_spec(b_f_col.shape)]
    args = [x, wt_qkv, wt_f, b_f_col]
    aliases = {}
    if stacked is not None:
        aliases = {len(args) + i: o for i, o in enumerate(stacked_outputs)}
        in_specs += [pl.BlockSpec(memory_space=pl.ANY)] * len(stacked)
        args += list(stacked)
    return pl.pallas_call(
        functools.partial(_inproj_prompt_kernel, width=width, n_heads=n_heads, q_scale=q_scale),
        grid=(m // tm,),
        in_specs=in_specs,
        out_specs=out_specs,
        out_shape=out_shape,
        input_output_aliases=aliases,
        compiler_params=_cparams(1),
        name="inproj_prompt",
    )(*args)


def _inproj_decode_kernel(x_ref, wt_ref, wft_ref, bf_ref,
                          qa_ref, qb_ref, ka_ref, va_ref, kb_ref, vb_ref,
                          kab_ref, vab_ref, kbb_ref, vbb_ref, logf_ref, *, width, n_heads, q_scale):
    xb = x_ref[...].astype(BF16)

    def proj(i):
        return _dot_nt(xb, wt_ref[i * width:(i + 1) * width, :])

    qa_ref[...] = (proj(0) * q_scale).astype(BF16)
    qb_ref[...] = (proj(3) * q_scale).astype(BF16)
    for i, (full_ref, half_ref) in ((1, (ka_ref, kab_ref)), (2, (va_ref, vab_ref)),
                                    (4, (kb_ref, kbb_ref)), (5, (vb_ref, vbb_ref))):
        val = proj(i)
        full_ref[...] = val
        half_ref[...] = val.astype(BF16)
    z = _dot_nt(xb, wft_ref[...]) + bf_ref[...]
    logf_ref[...] = _log_sigmoid(z)[:, :n_heads]


def _inproj_decode(x, wt_qkv, wt_f, b_f_row, *, n_heads, q_scale):
    m, d = x.shape
    width = wt_qkv.shape[0] // 6
    out_shape = ([jax.ShapeDtypeStruct((m, width), BF16)] * 2
                 + [jax.ShapeDtypeStruct((m, width), F32)] * 4
                 + [jax.ShapeDtypeStruct((m, width), BF16)] * 4
                 + [jax.ShapeDtypeStruct((m, n_heads), F32)])
    return pl.pallas_call(
        functools.partial(_inproj_decode_kernel, width=width, n_heads=n_heads, q_scale=q_scale),
        out_shape=out_shape,
        compiler_params=pltpu.CompilerParams(vmem_limit_bytes=VMEM_LIMIT),
        name="inproj_decode",
    )(x, wt_qkv, wt_f, b_f_row)


SCAN_BLOCK = 256


def _split3(a):
    a1 = a.astype(BF16)
    r1 = a - a1.astype(F32)
    a2 = r1.astype(BF16)
    a3 = (r1 - a2.astype(F32)).astype(BF16)
    return a1, a2, a3


def _cumsum_kernel(lf_ref, c_ref, *, n_blk, scale):
    rows = lf_ref.shape[0]
    r = lax.broadcasted_iota(jnp.int32, (SCAN_BLOCK, SCAN_BLOCK), 0)
    c = lax.broadcasted_iota(jnp.int32, (SCAN_BLOCK, SCAN_BLOCK), 1)
    upper = jnp.where(r <= c, 1.0, 0.0).astype(BF16)
    carry = jnp.zeros((rows, 1), F32)
    for j in range(n_blk):
        sl = slice(j * SCAN_BLOCK, (j + 1) * SCAN_BLOCK)
        a1, a2, a3 = _split3(lf_ref[:, sl])
        blk = _dot(a1, upper) + _dot(a2, upper) + _dot(a3, upper) + carry
        c_ref[:, sl] = blk * scale
        carry = blk[:, SCAN_BLOCK - 1:SCAN_BLOCK]


def _cumsum_rows(lf_rows, scale):
    rows, t = lf_rows.shape
    assert t % SCAN_BLOCK == 0
    return pl.pallas_call(
        functools.partial(_cumsum_kernel, n_blk=t // SCAN_BLOCK, scale=scale),
        out_shape=jax.ShapeDtypeStruct((rows, t), F32),
        compiler_params=pltpu.CompilerParams(vmem_limit_bytes=VMEM_LIMIT),
        name="logf_cumsum",
    )(lf_rows)


def _lambda_scalar(lq1, lk1, lq2, lk2, lambda_init):
    return (jnp.exp(jnp.sum(lq1 * lk1, axis=1, keepdims=True))
            - jnp.exp(jnp.sum(lq2 * lk2, axis=1, keepdims=True)) + lambda_init)


def _head_rms(o, g_row, lambda_init):
    o = o * lax.rsqrt(jnp.mean(o * o, axis=-1, keepdims=True) + RMS_EPS)
    return o * g_row * (1.0 - lambda_init)


def _alibi_slope(h, n_heads):
    return 2.0 ** (-8.0 * (h + 1) / n_heads)


def _sublane_half_masks(cols):
    row = lax.broadcasted_iota(jnp.int32, (LANES, cols), 0)
    lo = jnp.where(row < HD_A, 1.0, 0.0)
    return lo, 1.0 - lo


SUM_ROWS = 16


def _with_sum_rows(v_t):
    return jnp.concatenate([v_t, jnp.ones((SUM_ROWS, v_t.shape[1]), BF16)], axis=0)


def _flash_step_refs(i, t, shift, v_aug, m_ref, acc_ref):
    m = m_ref[i]
    m_new = jnp.maximum(m, jnp.max(t, axis=0, keepdims=True) + shift)
    p = jnp.exp2(t - (m_new - shift)).astype(BF16)
    acc_ref[i] = jnp.exp2(m - m_new) * acc_ref[i] + _dot(v_aug, p)
    m_ref[i] = m_new


def _flash_init_refs(m_ref, acc_ref):
    m_ref[...] = jnp.full(m_ref.shape, NEG_INF, F32)
    acc_ref[...] = jnp.zeros(acc_ref.shape, F32)


def _flash_result(i, d, acc_ref):
    acc = acc_ref[i]
    return acc[:d] / acc[d:d + 1]


def _stream_scratch(n_streams, d, tq):
    return [pltpu.VMEM((n_streams, LANES, tq), BF16), pltpu.VMEM((n_streams, 1, tq), F32),
            pltpu.VMEM((n_streams, d + SUM_ROWS, tq), F32)]


def _fox_prompt_kernel(qt_ref, k_ref, vt_ref, crow_ref, ccol_ref, o_ref, qm_ref, m_ref, acc_ref,
                       *, tq, tk, n_heads):
    qi = pl.program_id(1)
    n_diag = tq // tk
    lo, hi = _sublane_half_masks(tq)
    for h in range(n_heads):
        blk = slice((h // 2) * LANES, (h // 2 + 1) * LANES)
        qm_ref[h] = (qt_ref[0, blk, :].astype(F32) * (lo, hi)[h % 2]).astype(BF16)
    _flash_init_refs(m_ref, acc_ref)

    def tile(ks, mask):
        ck = ccol_ref[0, pl.ds(ks, tk), :]
        v_all = vt_ref[0, :, pl.ds(ks, tk)]
        scores = []
        for h in range(n_heads):
            blk = slice((h // 2) * LANES, (h // 2 + 1) * LANES)
            t = _dot(k_ref[pl.ds(ks, tk), blk], qm_ref[h]) - ck[:, h:h + 1]
            if mask is not None:
                t = jnp.where(mask, t, NEG_INF)
            scores.append(t)
        for h in range(n_heads):
            _flash_step_refs(h, scores[h], crow_ref[0, h:h + 1, :],
                             _with_sum_rows(v_all[h * HD_A:(h + 1) * HD_A]), m_ref, acc_ref)

    def full_tile(j, carry):
        tile(pl.multiple_of(j * tk, tk), None)
        return carry

    lax.fori_loop(0, qi * n_diag, full_tile, 0)
    r = lax.broadcasted_iota(jnp.int32, (tk, tq), 0)
    c = lax.broadcasted_iota(jnp.int32, (tk, tq), 1)
    for d in range(n_diag):
        tile(pl.multiple_of(qi * tq + d * tk, tk), (r + d * tk) <= c)
    for pair in range(n_heads // 2):
        blk = slice(pair * LANES, (pair + 1) * LANES)
        o_t = jnp.concatenate([_flash_result(2 * pair, HD_A, acc_ref),
                               _flash_result(2 * pair + 1, HD_A, acc_ref)], axis=0)
        o_ref[:, blk] = o_t.T.astype(BF16)


def _fox_prompt(qt, k, vt, c_row, c_col, *, batch, seq, tq, tk, n_heads):
    width = n_heads * HD_A
    nq = seq // tq
    return pl.pallas_call(
        functools.partial(_fox_prompt_kernel, tq=tq, tk=tk, n_heads=n_heads),
        grid=(batch, nq),
        in_specs=[pl.BlockSpec((1, width, tq), lambda b, i: (b, 0, i)),
                  pl.BlockSpec((seq, width), lambda b, i: (b, 0)),
                  pl.BlockSpec((1, width, seq), lambda b, i: (b, 0, 0)),
                  pl.BlockSpec((1, n_heads, tq), lambda b, i: (b, 0, i)),
                  pl.BlockSpec((1, seq, n_heads), lambda b, i: (b, 0, 0))],
        out_specs=pl.BlockSpec((tq, width), lambda b, i: (b * nq + i, 0)),
        out_shape=jax.ShapeDtypeStruct((batch * seq, width), BF16),
        scratch_shapes=_stream_scratch(n_heads, HD_A, tq),
        compiler_params=_cparams(2),
        name="fox_prompt",
    )(qt, k, vt, c_row, c_col)


def _diff_prompt_kernel(qt_ref, k_ref, vt_ref, lq1_ref, lk1_ref, lq2_ref, lk2_ref, g_ref, o_ref,
                        qm_ref, m_ref, acc_ref, *, tq, tk, n_heads, lambda_init):
    qi = pl.program_id(1)
    n_diag = tq // tk
    lo, hi = _sublane_half_masks(tq)
    for i in range(2 * n_heads):
        blk = slice((i // 2) * LANES, (i // 2 + 1) * LANES)
        qm_ref[i] = (qt_ref[0, blk, :].astype(F32) * (lo, hi)[i % 2]).astype(BF16)
    _flash_init_refs(m_ref, acc_ref)
    r = lax.broadcasted_iota(jnp.int32, (tk, tq), 0)
    c = lax.broadcasted_iota(jnp.int32, (tk, tq), 1)
    rel = (r - c).astype(F32)

    def tile(ks, bias_of, shift_of, mask):
        k_all = k_ref[pl.ds(ks, tk), :]
        v_all = vt_ref[0, :, pl.ds(ks, tk)]
        scores = []
        for h in range(n_heads):
            blk = slice(h * LANES, (h + 1) * LANES)
            bias = bias_of(_alibi_slope(h, n_heads) * LOG2E)
            for comp in range(2):
                t = _dot(k_all[:, blk], qm_ref[2 * h + comp]) + bias
                if mask is not None:
                    t = jnp.where(mask, t, NEG_INF)
                scores.append(t)
        for i in range(2 * n_heads):
            h = i // 2
            _flash_step_refs(i, scores[i], shift_of(_alibi_slope(h, n_heads) * LOG2E),
                             _with_sum_rows(v_all[h * LANES:(h + 1) * LANES]), m_ref, acc_ref)

    def full_tile(j, carry):
        off = (j * tk - qi * tq).astype(F32)
        tile(pl.multiple_of(j * tk, tk), lambda s2: s2 * rel, lambda s2: s2 * off, None)
        return carry

    lax.fori_loop(0, qi * n_diag, full_tile, 0)
    q_chunk = lax.shift_right_logical(c, CHUNK_SHIFT)
    for d in range(n_diag):
        dist = jnp.abs(rel + float(d * tk))
        visible = lax.shift_right_logical(r + d * tk, CHUNK_SHIFT) <= q_chunk
        tile(pl.multiple_of(qi * tq + d * tk, tk), lambda s2: -s2 * dist, lambda s2: 0.0, visible)
    lam = _lambda_scalar(lq1_ref[...], lk1_ref[...], lq2_ref[...], lk2_ref[...], lambda_init)
    for h in range(n_heads):
        blk = slice(h * LANES, (h + 1) * LANES)
        o = (_flash_result(2 * h, LANES, acc_ref) - lam * _flash_result(2 * h + 1, LANES, acc_ref)).T
        o_ref[:, blk] = _head_rms(o, g_ref[:, blk], lambda_init).astype(BF16)


def _diff_prompt(qt, k, vt, lams, g, *, batch, seq, tq, tk, n_heads, lambda_init):
    width = n_heads * 2 * DH_B
    nq = seq // tq
    return pl.pallas_call(
        functools.partial(_diff_prompt_kernel, tq=tq, tk=tk, n_heads=n_heads,
                          lambda_init=lambda_init),
        grid=(batch, nq),
        in_specs=[pl.BlockSpec((1, width, tq), lambda b, i: (b, 0, i)),
                  pl.BlockSpec((seq, width), lambda b, i: (b, 0)),
                  pl.BlockSpec((1, width, seq), lambda b, i: (b, 0, 0))]
                 + [_const_spec(a.shape) for a in lams] + [_const_spec(g.shape)],
        out_specs=pl.BlockSpec((tq, width), lambda b, i: (b * nq + i, 0)),
        out_shape=jax.ShapeDtypeStruct((batch * seq, width), BF16),
        scratch_shapes=_stream_scratch(2 * n_heads, LANES, tq),
        compiler_params=_cparams(2),
        name="diff_prompt",
    )(qt, k, vt, *lams, g)


def _lane_half_masks(rows):
    lane = lax.broadcasted_iota(jnp.int32, (rows, LANES), 1)
    lo = jnp.where(lane < HD_A, 1.0, 0.0)
    return lo, 1.0 - lo


def _masked_q(q2, mask):
    return (q2.astype(F32) * mask).astype(BF16)


def _flash_init(m_ref, l_ref, acc_ref):
    m_ref[...] = jnp.full(m_ref.shape, NEG_INF, F32)
    l_ref[...] = jnp.zeros(l_ref.shape, F32)
    acc_ref[...] = jnp.zeros(acc_ref.shape, F32)


def _flash_step(s, pv, m_ref, l_ref, acc_ref):
    m_prev = m_ref[...]
    m_new = jnp.maximum(m_prev, jnp.max(s, axis=1, keepdims=True))
    alpha = jnp.exp(m_prev - m_new)
    p = jnp.exp(s - m_new)
    l_ref[...] = alpha * l_ref[...] + jnp.sum(p, axis=1, keepdims=True)
    acc_ref[...] = alpha * acc_ref[...] + pv(p.astype(BF16))
    m_ref[...] = m_new


def _fox_decode_kernel(q_ref, kc_ref, vc_ref, kn_ref, vn_ref, ccol_ref, crow_ref, o_ref,
                       m_ref, l_ref, acc_ref, *, tq, tk, n_kv, n_heads):
    j = pl.program_id(1)
    lo, hi = _lane_half_masks(tq)

    @pl.when(j == 0)
    def _():
        _flash_init(m_ref, l_ref, acc_ref)

    def head_loop(scores_of, pv_of, bias_of, mask):
        pending = []
        for pair in range(n_heads // 2):
            blk = slice(pair * LANES, (pair + 1) * LANES)
            q2 = q_ref[:, blk]
            scores = scores_of(blk)
            pv = pv_of(blk)
            for half, hm in enumerate((lo, hi)):
                s = scores(_masked_q(q2, hm)) + bias_of(2 * pair + half)
                if mask is not None:
                    s = jnp.where(mask, s, NEG_INF)
                pending.append((s, pv))
        for h, (s, pv) in enumerate(pending):
            _flash_step(s, pv, m_ref.at[h], l_ref.at[h], acc_ref.at[h])

    @pl.when(j < n_kv)
    def _():
        ks = pl.multiple_of(j * tk, tk)

        def scores_of(blk):
            kt = kc_ref[0, 0, blk, :].astype(BF16)
            return lambda qm: _dot(qm, kt)

        def pv_of(blk):
            vt = vc_ref[0, 0, blk, :].astype(BF16)
            return lambda p: _dot_nt(p, vt)

        head_loop(scores_of, pv_of,
                  lambda h: ccol_ref[0, :, h:h + 1] - crow_ref[0, h:h + 1, pl.ds(ks, tk)], None)

    @pl.when(j == n_kv)
    def _():
        past = n_kv * tk
        q_pos = lax.broadcasted_iota(jnp.int32, (tq, tq), 0)
        k_pos = lax.broadcasted_iota(jnp.int32, (tq, tq), 1)
        head_loop(lambda blk: (lambda qm: _dot_nt(qm, kn_ref[:, blk])),
                  lambda blk: (lambda p: _dot(p, vn_ref[:, blk])),
                  lambda h: ccol_ref[0, :, h:h + 1] - crow_ref[0, h:h + 1, past:past + tq],
                  k_pos <= q_pos)
        for pair in range(n_heads // 2):
            blk = slice(pair * LANES, (pair + 1) * LANES)
            o0 = acc_ref[2 * pair] / l_ref[2 * pair]
            o1 = acc_ref[2 * pair + 1] / l_ref[2 * pair + 1]
            o_ref[:, blk] = (o0 * lo + o1 * hi).astype(BF16)


def _fox_decode(q, kt_cache, vt_cache, k_new, v_new, c_col, c_row, *, layer, tq, tk, n_heads):
    _, batch, width, past = kt_cache.shape
    n_kv = past // tk
    new_blk = pl.BlockSpec((tq, width), lambda b, j: (b, 0))
    cache_blk = pl.BlockSpec((1, 1, width, tk),
                             lambda b, j: (layer, b, 0, jnp.minimum(j, n_kv - 1)))
    return pl.pallas_call(
        functools.partial(_fox_decode_kernel, tq=tq, tk=tk, n_kv=n_kv, n_heads=n_heads),
        grid=(batch, n_kv + 1),
        in_specs=[new_blk, cache_blk, cache_blk, new_blk, new_blk,
                  pl.BlockSpec((1, tq, n_heads), lambda b, j: (b, past // tq, 0)),
                  pl.BlockSpec((1, n_heads, c_row.shape[2]), lambda b, j: (b, 0, 0))],
        out_specs=new_blk,
        out_shape=jax.ShapeDtypeStruct((batch * tq, width), BF16),
        scratch_shapes=[pltpu.VMEM((n_heads, tq, 1), F32), pltpu.VMEM((n_heads, tq, 1), F32),
                        pltpu.VMEM((n_heads, tq, LANES), F32)],
        compiler_params=_cparams(2),
        name="fox_decode",
    )(q, kt_cache, vt_cache, k_new, v_new, c_col, c_row)


def _diff_decode_kernel(q_ref, kc_ref, vc_ref, kn_ref, vn_ref, lq1_ref, lk1_ref, lq2_ref, lk2_ref,
                        g_ref, o_ref, m_ref, l_ref, acc_ref,
                        *, tq, tk, n_kv, n_heads, lambda_init):
    j = pl.program_id(1)
    lo, hi = _lane_half_masks(tq)
    past = n_kv * tk
    q_abs = (past + lax.broadcasted_iota(jnp.int32, (tq, 1), 0)).astype(F32)

    @pl.when(j == 0)
    def _():
        _flash_init(m_ref, l_ref, acc_ref)

    def head_loop(scores_of, pv_of, dist):
        pending = []
        for h in range(n_heads):
            blk = slice(h * LANES, (h + 1) * LANES)
            slope = _alibi_slope(h, n_heads)
            q2 = q_ref[:, blk]
            scores = scores_of(blk)
            pv = pv_of(h)
            for hm in (lo, hi):
                pending.append((scores(_masked_q(q2, hm)) - slope * dist, pv))
        for idx, (s, pv) in enumerate(pending):
            _flash_step(s, pv, m_ref.at[idx], l_ref.at[idx], acc_ref.at[idx])

    @pl.when(j < n_kv)
    def _():
        k_abs = (j * tk + lax.broadcasted_iota(jnp.int32, (1, tk), 1)).astype(F32)

        def scores_of(blk):
            kt = kc_ref[0, 0, blk, :].astype(BF16)
            return lambda qm: _dot(qm, kt)

        def pv_of(h):
            v2 = vc_ref[0, 0, :, h, :].astype(BF16)
            return lambda p: _dot(p, v2)

        head_loop(scores_of, pv_of, jnp.abs(q_abs - k_abs))

    @pl.when(j == n_kv)
    def _():
        assert past % CHUNK == 0 and tq <= CHUNK
        k_abs = (past + lax.broadcasted_iota(jnp.int32, (1, tq), 1)).astype(F32)
        head_loop(lambda blk: (lambda qm: _dot_nt(qm, kn_ref[:, blk])),
                  lambda h: (lambda p: _dot(p, vn_ref[:, h * LANES:(h + 1) * LANES])),
                  jnp.abs(q_abs - k_abs))
        lam = _lambda_scalar(lq1_ref[...], lk1_ref[...], lq2_ref[...], lk2_ref[...], lambda_init)
        for h in range(n_heads):
            blk = slice(h * LANES, (h + 1) * LANES)
            o = acc_ref[2 * h] / l_ref[2 * h] - lam * (acc_ref[2 * h + 1] / l_ref[2 * h + 1])
            o_ref[:, blk] = _head_rms(o, g_ref[:, blk], lambda_init).astype(BF16)


def _diff_decode(q, kt_cache, v_cache, k_new, v_new, lams, g, *, layer, tq, tk, n_heads,
                 lambda_init):
    _, batch, width, past = kt_cache.shape
    n_kv = past // tk
    new_blk = pl.BlockSpec((tq, width), lambda b, j: (b, 0))
    return pl.pallas_call(
        functools.partial(_diff_decode_kernel, tq=tq, tk=tk, n_kv=n_kv, n_heads=n_heads,
                          lambda_init=lambda_init),
        grid=(batch, n_kv + 1),
        in_specs=[new_blk,
                  pl.BlockSpec((1, 1, width, tk),
                               lambda b, j: (layer, b, 0, jnp.minimum(j, n_kv - 1))),
                  pl.BlockSpec((1, 1, tk, n_heads, LANES),
                               lambda b, j: (layer, b, jnp.minimum(j, n_kv - 1), 0, 0)),
                  new_blk, new_blk]
                 + [_const_spec(a.shape) for a in lams] + [_const_spec(g.shape)],
        out_specs=new_blk,
        out_shape=jax.ShapeDtypeStruct((batch * tq, width), BF16),
        scratch_shapes=[pltpu.VMEM((2 * n_heads, tq, 1), F32),
                        pltpu.VMEM((2 * n_heads, tq, 1), F32),
                        pltpu.VMEM((2 * n_heads, tq, LANES), F32)],
        compiler_params=_cparams(2),
        name="diff_decode",
    )(q, kt_cache, v_cache, k_new, v_new, *lams, g)


def _post_kernel(x_ref, oa_ref, ob_ref, wgt_ref, wba_ref, wbb_ref, wo_ref, g_ref, b_ref, o_ref,
                 *, alpha):
    x = x_ref[...]
    xb = x.astype(BF16)
    d = x.shape[1]
    merged = _sigmoid(_dot_nt(xb, wgt_ref[:d, :])) * _dot(oa_ref[...], wba_ref[...])
    merged = merged + _sigmoid(_dot_nt(xb, wgt_ref[d:, :])) * _dot(ob_ref[...], wbb_ref[...])
    y = alpha * x + _dot(merged.astype(BF16), wo_ref[...])
    o_ref[...] = _layer_norm(y, g_ref[...], b_ref[...])


def _post(x, oa, ob, wt_gate, w_ba, w_bb, w_o, g, b, *, tm, alpha):
    m, d = x.shape
    row = lambda i: (i, 0)
    return pl.pallas_call(
        functools.partial(_post_kernel, alpha=alpha),
        grid=(m // tm,),
        in_specs=[pl.BlockSpec((tm, d), row), pl.BlockSpec((tm, oa.shape[1]), row),
                  pl.BlockSpec((tm, ob.shape[1]), row)]
                 + [_const_spec(a.shape) for a in (wt_gate, w_ba, w_bb, w_o, g, b)],
        out_specs=pl.BlockSpec((tm, d), row),
        out_shape=jax.ShapeDtypeStruct((m, d), F32),
        compiler_params=_cparams(1),
        name="merge_outproj_ln",
    )(x, oa, ob, wt_gate, w_ba, w_bb, w_o, g, b)


def _ffn_kernel(x_ref, p_ref, wg_ref, wu_ref, wd_ref, wpg_ref, wpp_ref, g_ref, b_ref, o_ref,
                *, alpha):
    x = x_ref[...]
    xb = x.astype(BF16)
    hg = _dot(xb, wg_ref[...])
    hidden = (hg * _sigmoid(hg) * _dot(xb, wu_ref[...])).astype(BF16)
    y = alpha * x + _dot(hidden, wd_ref[...])
    ple = _sigmoid(_dot(xb, wpg_ref[...])) * _dot(p_ref[...].astype(BF16), wpp_ref[...])
    o_ref[...] = _layer_norm(y + ple, g_ref[...], b_ref[...])


def _ffn(x, p, w_g, w_u, w_d, w_pg, w_pp, g, b, *, tm, alpha):
    m, d = x.shape
    row = lambda i: (i, 0)
    return pl.pallas_call(
        functools.partial(_ffn_kernel, alpha=alpha),
        grid=(m // tm,),
        in_specs=[pl.BlockSpec((tm, d), row), pl.BlockSpec((tm, p.shape[1]), row)]
                 + [_const_spec(a.shape) for a in (w_g, w_u, w_d, w_pg, w_pp, g, b)],
        out_specs=pl.BlockSpec((tm, d), row),
        out_shape=jax.ShapeDtypeStruct((m, d), F32),
        compiler_params=_cparams(1),
        name="swiglu_ple_ln",
    )(x, p, w_g, w_u, w_d, w_pg, w_pp, g, b)


def _row_tile(m):
    for tm in (512, 256, 128):
        if m % tm == 0:
            return tm
    raise ValueError(f"row count {m} is not a multiple of 128")


def kernel(x_prompt, x_sample, p_prompt, p_sample, cache_fox_k, cache_fox_v, cache_fox_logf, cache_diff_k, cache_diff_v, w_in, b_forget, lambda_q1, lambda_k1, lambda_q2, lambda_k2, diff_norm_g, w_branch_fox, w_branch_diff, w_out, ln1_g, ln1_b, w_ffn_gate, w_ffn_up, w_ffn_down, w_ple_gate, w_ple_proj, ln2_g, ln2_b):
    batch, seq, d_model = x_prompt.shape
    dec_batch, dec_seq, _ = x_sample.shape
    depth = w_in.shape[0]
    past = cache_fox_k.shape[2]
    h_a = cache_fox_k.shape[3]
    h_b = cache_diff_k.shape[3]
    w_a = h_a * HD_A
    w_b = h_b * 2 * DH_B
    assert w_a == w_b and w_in.shape[2] == 3 * w_a + h_a + 3 * w_b + 2 * d_model
    assert HD_A == DH_B
    alpha = (2 * depth) ** 0.25
    m_p = batch * seq
    m_s = dec_batch * dec_seq
    tm_p = _row_tile(seq)
    tm_s = _row_tile(m_s)
    tq, tk = min(256, seq), min(256, seq)
    tk_dec = min(1024, past)
    assert seq % tq == 0 and tq % tk == 0 and tk % CHUNK == 0 and past % tk_dec == 0
    t_dec = past + dec_seq
    t_dec_pad = -(-t_dec // SCAN_BLOCK) * SCAN_BLOCK
    qk_scale = HD_A ** -0.5
    f_pad = 16

    xp = x_prompt.reshape(m_p, d_model)
    xs = x_sample.reshape(m_s, d_model)

    wt_in = jnp.swapaxes(w_in, 1, 2)
    kt_cache_a = jnp.transpose(cache_fox_k, (0, 1, 3, 4, 2)).reshape(depth, dec_batch, w_a, past)
    vt_cache_a = jnp.transpose(cache_fox_v, (0, 1, 3, 4, 2)).reshape(depth, dec_batch, w_a, past)
    kt_cache_b = jnp.transpose(cache_diff_k, (0, 1, 3, 4, 5, 2)).reshape(depth, dec_batch, w_b, past)
    lf_cache = jnp.swapaxes(cache_fox_logf, 2, 3).astype(F32)

    o_f = 3 * w_a
    o_qb = o_f + h_a
    o_ga = o_qb + 3 * w_b

    stacked = None
    new_s = [[] for _ in range(5)]
    for l in range(depth):
        lambda_init = 0.8 - 0.6 * math.exp(-0.3 * l)
        wt = wt_in[l]
        wt_qkv = jnp.concatenate([wt[:o_f], wt[o_qb:o_ga]], axis=0).astype(BF16)
        wt_f = jnp.pad(wt[o_f:o_qb], ((0, f_pad - h_a), (0, 0))).astype(BF16)
        b_f = jnp.pad(b_forget[l], (0, f_pad - h_a))
        wt_gate = wt[o_ga:].astype(BF16)

        (qat, ka, kat, vat, vat_b, qbt, kb, kbt, vb, vbt_b, logf_p) = _inproj_prompt(
            xp, wt_qkv, wt_f, b_f.reshape(f_pad, 1), stacked, layer=l, depth=depth, batch=batch,
            seq=seq, tm=tm_p, n_heads=h_a, q_scale=qk_scale * LOG2E)
        stacked = (kat, vat, kbt, vb, logf_p)
        (qa_s, qb_s, ka_s, va_s, kb_s, vb_s, ka_sb, va_sb, kb_sb, vb_sb, logf_s) = _inproj_decode(
            xs, wt_qkv, wt_f, b_f.reshape(1, f_pad), n_heads=h_a, q_scale=qk_scale)

        c_row_p = _cumsum_rows(logf_p[l].reshape(batch * h_a, seq), LOG2E).reshape(batch, h_a, seq)
        c_col_p = jnp.swapaxes(c_row_p, 1, 2)
        lf_s = jnp.swapaxes(logf_s.reshape(dec_batch, dec_seq, h_a), 1, 2)
        lf_all = jnp.concatenate([lf_cache[l], lf_s], axis=2)
        lf_all = jnp.pad(lf_all, ((0, 0), (0, 0), (0, t_dec_pad - t_dec)))
        c_row_s = _cumsum_rows(lf_all.reshape(dec_batch * h_a, t_dec_pad), 1.0)
        c_row_s = c_row_s.reshape(dec_batch, h_a, t_dec_pad)
        c_col_s = jnp.swapaxes(c_row_s, 1, 2)

        lams = [a[l].reshape(1, DH_B) for a in (lambda_q1, lambda_k1, lambda_q2, lambda_k2)]
        g_diff = diff_norm_g[l].reshape(1, w_b)

        oa_p = _fox_prompt(qat, ka, vat_b, c_row_p, c_col_p,
                           batch=batch, seq=seq, tq=tq, tk=tk, n_heads=h_a)
        ob_p = _diff_prompt(qbt, kb, vbt_b, lams, g_diff, batch=batch, seq=seq, tq=tq, tk=tk,
                            n_heads=h_b, lambda_init=lambda_init)
        oa_s = _fox_decode(qa_s, kt_cache_a, vt_cache_a, ka_sb, va_sb, c_col_s, c_row_s,
                           layer=l, tq=dec_seq, tk=tk_dec, n_heads=h_a)
        ob_s = _diff_decode(qb_s, kt_cache_b, cache_diff_v, kb_sb, vb_sb, lams, g_diff,
                            layer=l, tq=dec_seq, tk=tk_dec, n_heads=h_b, lambda_init=lambda_init)

        dense_w = (wt_gate, w_branch_fox[l].astype(BF16), w_branch_diff[l].astype(BF16),
                   w_out[l].astype(BF16), ln1_g[l].reshape(1, d_model), ln1_b[l].reshape(1, d_model))
        ffn_w = (w_ffn_gate[l].astype(BF16), w_ffn_up[l].astype(BF16), w_ffn_down[l].astype(BF16),
                 w_ple_gate[l].astype(BF16), w_ple_proj[l].astype(BF16),
                 ln2_g[l].reshape(1, d_model), ln2_b[l].reshape(1, d_model))
        xp = _post(xp, oa_p, ob_p, *dense_w, tm=tm_p, alpha=alpha)
        xp = _ffn(xp, p_prompt[l].reshape(m_p, -1), *ffn_w, tm=tm_p, alpha=alpha)
        xs = _post(xs, oa_s, ob_s, *dense_w, tm=tm_s, alpha=alpha)
        xs = _ffn(xs, p_sample[l].reshape(m_s, -1), *ffn_w, tm=tm_s, alpha=alpha)

        for lst, r in zip(new_s, (ka_s, va_s, logf_s, kb_s, vb_s)):
            lst.append(r)

    kat, vat, kbt, vb, logf_p = stacked
    ka_s, va_s, logf_s, kb_s, vb_s = [jnp.stack(a) for a in new_s]
    fox_k_p = jnp.transpose(kat.reshape(depth, batch, h_a, HD_A, seq), (0, 1, 4, 2, 3))
    fox_v_p = jnp.transpose(vat.reshape(depth, batch, h_a, HD_A, seq), (0, 1, 4, 2, 3))
    fox_lf_p = jnp.swapaxes(logf_p, 2, 3)
    diff_k_p = jnp.transpose(kbt.reshape(depth, batch, h_b, 2, DH_B, seq), (0, 1, 5, 2, 3, 4))
    return (xp.reshape(batch, seq, d_model), xs.reshape(dec_batch, dec_seq, d_model),
            fox_k_p, fox_v_p, fox_lf_p, diff_k_p, vb,
            ka_s.reshape(depth, dec_batch, dec_seq, h_a, HD_A),
            va_s.reshape(depth, dec_batch, dec_seq, h_a, HD_A),
            logf_s.reshape(depth, dec_batch, dec_seq, h_a),
            kb_s.reshape(depth, dec_batch, dec_seq, h_b, 2, DH_B),
            vb_s.reshape(depth, dec_batch, dec_seq, h_b, 2 * DH_B))
```

```python
import functools
import math

import jax
import jax.numpy as jnp
import numpy as np
from jax import lax
from jax.experimental import pallas as pl
from jax.experimental.pallas import tpu as pltpu

F32 = jnp.float32
BF16 = jnp.bfloat16

HD_A = 64
DH_B = 64
CHUNK = 64
CHUNK_SHIFT = 6
LN_EPS = 1e-5
RMS_EPS = 1e-5
NEG_INF = -1e30
LOG2E = math.log2(math.e)
LANES = 128
VMEM_LIMIT = 56 * 1024 * 1024


def _cparams(n_axes):
    return pltpu.CompilerParams(dimension_semantics=("arbitrary",) * n_axes,
                                vmem_limit_bytes=VMEM_LIMIT)


def _const_spec(shape):
    zeros = (0,) * len(shape)
    return pl.BlockSpec(shape, lambda *_: zeros, pipeline_mode=pl.Buffered(1))


def _layer_spec(shape, layer):
    zeros = (0,) * (len(shape) - 1)
    return pl.BlockSpec((None,) + tuple(shape[1:]), lambda *_: (layer,) + zeros,
                        pipeline_mode=pl.Buffered(1))


def _dot(a, b):
    return jnp.dot(a, b, preferred_element_type=F32)


def _dot_nt(a, b):
    return lax.dot_general(a, b, (((1,), (1,)), ((), ())), preferred_element_type=F32)


def _sigmoid(x):
    return 1.0 / (1.0 + jnp.exp(-x))


def _log_sigmoid(z):
    return jnp.minimum(z, 0.0) - jnp.log1p(jnp.exp(-jnp.abs(z)))


def _layer_norm(y, g, b):
    mu = jnp.mean(y, axis=-1, keepdims=True)
    yc = y - mu
    var = jnp.mean(yc * yc, axis=-1, keepdims=True)
    return yc * lax.rsqrt(var + LN_EPS) * g + b


def _inproj_prompt_kernel(x_ref, wt_ref, wft_ref, bf_ref, *refs, width, n_heads, q_scale):
    (qat_ref, ka_ref, kat_ref, vat_ref, vatb_ref,
     qbt_ref, kb_ref, kbt_ref, vb_ref, vbtb_ref, logf_ref) = refs[-11:]
    xb = x_ref[...].astype(BF16)

    def proj_t(i):
        return _dot_nt(wt_ref[i * width:(i + 1) * width, :], xb)

    qat_ref[0] = (proj_t(0) * q_scale).astype(BF16)
    kat = proj_t(1)
    kat_ref[0, 0] = kat
    ka_ref[...] = kat.T.astype(BF16)
    vat = proj_t(2)
    vat_ref[0, 0] = vat
    vatb_ref[0] = vat.astype(BF16)
    qbt_ref[0] = (proj_t(3) * q_scale).astype(BF16)
    kbt = proj_t(4)
    kbt_ref[0, 0] = kbt
    kb_ref[...] = kbt.T.astype(BF16)
    vbt = proj_t(5)
    vbtb_ref[0] = vbt.astype(BF16)
    vb = vbt.T
    for h in range(width // LANES):
        vb_ref[0, 0, :, h, :] = vb[:, h * LANES:(h + 1) * LANES]
    z = _dot_nt(wft_ref[...], xb) + bf_ref[...]
    logf_ref[0, 0] = _log_sigmoid(z)[:n_heads, :]


def _inproj_prompt(x, wt_qkv, wt_f, b_f_col, stacked, *, layer, depth, batch, seq, tm, n_heads,
                   q_scale):
    m, d = x.shape
    width = wt_qkv.shape[0] // 6
    nt = seq // tm
    row = lambda i: (i, 0)
    tr = lambda i: (i // nt, 0, i % nt)
    tr_l = lambda i: (layer, i // nt, 0, i % nt)
    t_blk = pl.BlockSpec((1, width, tm), tr)
    t_blk_l = pl.BlockSpec((1, 1, width, tm), tr_l)
    n_blk = pl.BlockSpec((tm, width), row)
    t_f32 = jax.ShapeDtypeStruct((depth, batch, width, seq), F32)
    t_b16 = jax.ShapeDtypeStruct((batch, width, seq), BF16)
    n_b16 = jax.ShapeDtypeStruct((m, width), BF16)
    vb_heads = width // LANES
    out_specs = [t_blk, n_blk, t_blk_l, t_blk_l, t_blk, t_blk, n_blk, t_blk_l,
                 pl.BlockSpec((1, 1, tm, vb_heads, LANES),
                              lambda i: (layer, i // nt, i % nt, 0, 0)),
                 t_blk, pl.BlockSpec((1, 1, n_heads, tm), tr_l)]
    out_shape = [t_b16, n_b16, t_f32, t_f32, t_b16, t_b16, n_b16, t_f32,
                 jax.ShapeDtypeStruct((depth, batch, seq, vb_heads, LANES), F32),
                 t_b16, jax.ShapeDtypeStruct((depth, batch, n_heads, seq), F32)]
    stacked_outputs = (2, 3, 7, 8, 10)
    in_specs = [pl.BlockSpec((tm, d), row), _const_spec(wt_qkv.shape), _const_spec(wt_f.shape),
                _const_spec(b_f_col.shape)]
    args = [x, wt_qkv, wt_f, b_f_col]
    aliases = {}
    if stacked is not None:
        aliases = {len(args) + i: o for i, o in enumerate(stacked_outputs)}
        in_specs += [pl.BlockSpec(memory_space=pl.ANY)] * len(stacked)
        args += list(stacked)
    return pl.pallas_call(
        functools.partial(_inproj_prompt_kernel, width=width, n_heads=n_heads, q_scale=q_scale),
        grid=(m // tm,),
        in_specs=in_specs,
        out_specs=out_specs,
        out_shape=out_shape,
        input_output_aliases=aliases,
        compiler_params=_cparams(1),
        name="inproj_prompt",
    )(*args)


def _inproj_decode_kernel(x_ref, wt_ref, wft_ref, bf_ref,
                          qa_ref, qb_ref, ka_ref, va_ref, kb_ref, vb_ref,
                          kab_ref, vab_ref, kbb_ref, vbb_ref, logf_ref, *, width, n_heads, q_scale):
    xb = x_ref[...].astype(BF16)

    def proj(i):
        return _dot_nt(xb, wt_ref[i * width:(i + 1) * width, :])

    qa_ref[...] = (proj(0) * q_scale).astype(BF16)
    qb_ref[...] = (proj(3) * q_scale).astype(BF16)
    for i, (full_ref, half_ref) in ((1, (ka_ref, kab_ref)), (2, (va_ref, vab_ref)),
                                    (4, (kb_ref, kbb_ref)), (5, (vb_ref, vbb_ref))):
        val = proj(i)
        full_ref[...] = val
        half_ref[...] = val.astype(BF16)
    z = _dot_nt(xb, wft_ref[...]) + bf_ref[...]
    logf_ref[...] = _log_sigmoid(z)[:, :n_heads]


def _inproj_decode(x, wt_qkv, wt_f, b_f_row, *, n_heads, q_scale):
    m, d = x.shape
    width = wt_qkv.shape[0] // 6
    out_shape = ([jax.ShapeDtypeStruct((m, width), BF16)] * 2
                 + [jax.ShapeDtypeStruct((m, width), F32)] * 4
                 + [jax.ShapeDtypeStruct((m, width), BF16)] * 4
                 + [jax.ShapeDtypeStruct((m, n_heads), F32)])
    return pl.pallas_call(
        functools.partial(_inproj_decode_kernel, width=width, n_heads=n_heads, q_scale=q_scale),
        out_shape=out_shape,
        compiler_params=pltpu.CompilerParams(vmem_limit_bytes=VMEM_LIMIT),
        name="inproj_decode",
    )(x, wt_qkv, wt_f, b_f_row)


SCAN_BLOCK = 256


def _split3(a):
    a1 = a.astype(BF16)
    r1 = a - a1.astype(F32)
    a2 = r1.astype(BF16)
    a3 = (r1 - a2.astype(F32)).astype(BF16)
    return a1, a2, a3


def _split3_const(x):
    pieces, rest = [], np.float32(x)
    for _ in range(3):
        p = np.float32(np.asarray(rest, dtype=jnp.bfloat16))
        pieces.append(float(p))
        rest = np.float32(rest - p)
    return pieces


def _cumsum_kernel(lf_ref, c_ref, *, n_blk, scale):
    rows = lf_ref.shape[0]
    r = lax.broadcasted_iota(jnp.int32, (SCAN_BLOCK, SCAN_BLOCK), 0)
    c = lax.broadcasted_iota(jnp.int32, (SCAN_BLOCK, SCAN_BLOCK), 1)
    upper = jnp.where(r <= c, 1.0, 0.0).astype(BF16)
    carry = jnp.zeros((rows, 1), F32)
    for j in range(n_blk):
        sl = slice(j * SCAN_BLOCK, (j + 1) * SCAN_BLOCK)
        a1, a2, a3 = _split3(lf_ref[:, sl])
        blk = _dot(a1, upper) + _dot(a2, upper) + _dot(a3, upper) + carry
        c_ref[:, sl] = blk * scale
        carry = blk[:, SCAN_BLOCK - 1:SCAN_BLOCK]


def _cumsum_rows(lf_rows, scale):
    rows, t = lf_rows.shape
    assert t % SCAN_BLOCK == 0
    return pl.pallas_call(
        functools.partial(_cumsum_kernel, n_blk=t // SCAN_BLOCK, scale=scale),
        out_shape=jax.ShapeDtypeStruct((rows, t), F32),
        compiler_params=pltpu.CompilerParams(vmem_limit_bytes=VMEM_LIMIT),
        name="logf_cumsum",
    )(lf_rows)


AUG_PIECES = 3


def _aug_lane(h, piece):
    return AUG_PIECES * (h % 2) + piece


def _cumsum_aug_kernel(lf_rows_ref, lf_cols_ref, place_ref, c_ref, aug_ref, *, n_blk, scale):
    rows = lf_rows_ref.shape[0]
    batch, _, width = aug_ref.shape
    r = lax.broadcasted_iota(jnp.int32, (SCAN_BLOCK, SCAN_BLOCK), 0)
    c = lax.broadcasted_iota(jnp.int32, (SCAN_BLOCK, SCAN_BLOCK), 1)
    upper = jnp.where(r <= c, 1.0, 0.0).astype(BF16)
    lower = jnp.where(c <= r, 1.0, 0.0).astype(BF16)
    carry_r = jnp.zeros((rows, 1), F32)
    carry_c = jnp.zeros((1, rows), F32)
    for j in range(n_blk):
        sl = slice(j * SCAN_BLOCK, (j + 1) * SCAN_BLOCK)
        a1, a2, a3 = _split3(lf_rows_ref[:, sl])
        blk_r = _dot(a1, upper) + _dot(a2, upper) + _dot(a3, upper) + carry_r
        c_ref[:, sl] = blk_r * scale
        carry_r = blk_r[:, SCAN_BLOCK - 1:SCAN_BLOCK]
        b1, b2, b3 = _split3(lf_cols_ref[sl, :])
        blk_c = _dot(lower, b1) + _dot(lower, b2) + _dot(lower, b3) + carry_c
        carry_c = blk_c[SCAN_BLOCK - 1:SCAN_BLOCK, :]
        pieces = jnp.concatenate(_split3(blk_c * scale), axis=1)
        placed = _dot(pieces, place_ref[...]).astype(BF16)
        for b in range(batch):
            aug_ref[b, sl, :] = placed[:, b * width:(b + 1) * width]


def _placement_matrix(batch, n_heads, width):
    rows = batch * n_heads
    place = np.zeros((AUG_PIECES * rows, batch * width), np.float32)
    for b in range(batch):
        for h in range(n_heads):
            for p in range(AUG_PIECES):
                place[p * rows + b * n_heads + h,
                      b * width + (h // 2) * LANES + _aug_lane(h, p)] = 1.0
    return jnp.asarray(place, BF16)


def _cumsum_aug(lf, scale, *, width):
    batch, n_heads, t = lf.shape
    assert t % SCAN_BLOCK == 0
    rows = batch * n_heads
    c_rows, aug = pl.pallas_call(
        functools.partial(_cumsum_aug_kernel, n_blk=t // SCAN_BLOCK, scale=scale),
        out_shape=[jax.ShapeDtypeStruct((rows, t), F32),
                   jax.ShapeDtypeStruct((batch, t, width), BF16)],
        compiler_params=pltpu.CompilerParams(vmem_limit_bytes=VMEM_LIMIT),
        name="logf_cumsum_aug",
    )(lf.reshape(rows, t), jnp.transpose(lf, (2, 0, 1)).reshape(t, rows),
      _placement_matrix(batch, n_heads, width))
    return c_rows.reshape(batch, n_heads, t), aug


def _lambda_scalar(lq1, lk1, lq2, lk2, lambda_init):
    return (jnp.exp(jnp.sum(lq1 * lk1, axis=1, keepdims=True))
            - jnp.exp(jnp.sum(lq2 * lk2, axis=1, keepdims=True)) + lambda_init)


def _head_rms(o, g_row, lambda_init):
    o = o * lax.rsqrt(jnp.mean(o * o, axis=-1, keepdims=True) + RMS_EPS)
    return o * g_row * (1.0 - lambda_init)


def _alibi_slope(h, n_heads):
    return 2.0 ** (-8.0 * (h + 1) / n_heads)


def _sublane_half_masks(cols):
    row = lax.broadcasted_iota(jnp.int32, (LANES, cols), 0)
    lo = jnp.where(row < HD_A, 1.0, 0.0)
    return lo, 1.0 - lo


SUM_ROWS = 16


def _with_sum_rows(v_t):
    return jnp.concatenate([v_t, jnp.ones((SUM_ROWS, v_t.shape[1]), BF16)], axis=0)


def _flash_step_refs(i, t, shift, v_aug, m_ref, acc_ref):
    m = m_ref[i]
    m_new = jnp.maximum(m, jnp.max(t, axis=0, keepdims=True) + shift)
    p = jnp.exp2(t - (m_new - shift)).astype(BF16)
    acc_ref[i] = jnp.exp2(m - m_new) * acc_ref[i] + _dot(v_aug, p)
    m_ref[i] = m_new


def _flash_init_refs(m_ref, acc_ref):
    m_ref[...] = jnp.full(m_ref.shape, NEG_INF, F32)
    acc_ref[...] = jnp.zeros(acc_ref.shape, F32)


def _flash_result(i, d, acc_ref):
    acc = acc_ref[i]
    return acc[:d] / acc[d:d + 1]


def _stream_scratch(n_streams, d, tq):
    return [pltpu.VMEM((n_streams, 2 * LANES, tq), BF16), pltpu.VMEM((n_streams, 1, tq), F32),
            pltpu.VMEM((n_streams, d + SUM_ROWS, tq), F32)]


def _fox_prompt_kernel(qt_ref, k_ref, vt_ref, crow_ref, aug_ref, o_ref, qm_ref, m_ref, acc_ref,
                       *, tq, tk, n_heads):
    qi = pl.program_id(1)
    n_diag = tq // tk
    lo, hi = _sublane_half_masks(tq)
    row = lax.broadcasted_iota(jnp.int32, (LANES, tq), 0)
    for h in range(n_heads):
        blk = slice((h // 2) * LANES, (h // 2 + 1) * LANES)
        qm_ref[h, :LANES] = (qt_ref[0, blk, :].astype(F32) * (lo, hi)[h % 2]).astype(BF16)
        picks = (row >= _aug_lane(h, 0)) & (row < _aug_lane(h, AUG_PIECES))
        qm_ref[h, LANES:] = jnp.where(picks, -1.0, 0.0).astype(BF16)
    _flash_init_refs(m_ref, acc_ref)

    def tile(ks, mask):
        v_all = vt_ref[0, :, pl.ds(ks, tk)]
        scores = []
        for h in range(n_heads):
            blk = slice((h // 2) * LANES, (h // 2 + 1) * LANES)
            k_aug = jnp.concatenate([k_ref[pl.ds(ks, tk), blk], aug_ref[0, pl.ds(ks, tk), blk]],
                                    axis=1)
            t = _dot(k_aug, qm_ref[h])
            if mask is not None:
                t = jnp.where(mask, t, NEG_INF)
            scores.append(t)
        for h in range(n_heads):
            _flash_step_refs(h, scores[h], crow_ref[0, h:h + 1, :],
                             _with_sum_rows(v_all[h * HD_A:(h + 1) * HD_A]), m_ref, acc_ref)

    def full_tile(j, carry):
        tile(pl.multiple_of(j * tk, tk), None)
        return carry

    lax.fori_loop(0, qi * n_diag, full_tile, 0)
    r = lax.broadcasted_iota(jnp.int32, (tk, tq), 0)
    c = lax.broadcasted_iota(jnp.int32, (tk, tq), 1)
    for d in range(n_diag):
        tile(pl.multiple_of(qi * tq + d * tk, tk), (r + d * tk) <= c)
    for pair in range(n_heads // 2):
        blk = slice(pair * LANES, (pair + 1) * LANES)
        o_t = jnp.concatenate([_flash_result(2 * pair, HD_A, acc_ref),
                               _flash_result(2 * pair + 1, HD_A, acc_ref)], axis=0)
        o_ref[:, blk] = o_t.T.astype(BF16)


def _fox_prompt(qt, k, vt, c_row, aug, *, batch, seq, tq, tk, n_heads):
    width = n_heads * HD_A
    nq = seq // tq
    return pl.pallas_call(
        functools.partial(_fox_prompt_kernel, tq=tq, tk=tk, n_heads=n_heads),
        grid=(batch, nq),
        in_specs=[pl.BlockSpec((1, width, tq), lambda b, i: (b, 0, i)),
                  pl.BlockSpec((seq, width), lambda b, i: (b, 0)),
                  pl.BlockSpec((1, width, seq), lambda b, i: (b, 0, 0)),
                  pl.BlockSpec((1, n_heads, tq), lambda b, i: (b, 0, i)),
                  pl.BlockSpec((1, seq, width), lambda b, i: (b, 0, 0))],
        out_specs=pl.BlockSpec((tq, width), lambda b, i: (b * nq + i, 0)),
        out_shape=jax.ShapeDtypeStruct((batch * seq, width), BF16),
        scratch_shapes=_stream_scratch(n_heads, HD_A, tq),
        compiler_params=_cparams(2),
        name="fox_prompt",
    )(qt, k, vt, c_row, aug)


def _diff_prompt_kernel(qt_ref, k_ref, vt_ref, lq1_ref, lk1_ref, lq2_ref, lk2_ref, g_ref, o_ref,
                        qm_ref, m_ref, acc_ref, *, tq, tk, n_heads, lambda_init):
    qi = pl.program_id(1)
    n_diag = tq // tk
    lo, hi = _sublane_half_masks(tq)
    row = lax.broadcasted_iota(jnp.int32, (LANES, tq), 0)
    for i in range(2 * n_heads):
        blk = slice((i // 2) * LANES, (i // 2 + 1) * LANES)
        qm_ref[i, :LANES] = (qt_ref[0, blk, :].astype(F32) * (lo, hi)[i % 2]).astype(BF16)
        s1, s2, s3 = _split3_const(_alibi_slope(i // 2, n_heads) * LOG2E)
        slope_rows = jnp.where(row == 0, s1, jnp.where(row == 1, s2, jnp.where(row == 2, s3, 0.0)))
        qm_ref[i, LANES:] = slope_rows.astype(BF16)
    _flash_init_refs(m_ref, acc_ref)
    assert tk <= 256
    lane = lax.broadcasted_iota(jnp.int32, (tk, LANES), 1)
    key_off = lax.broadcasted_iota(jnp.int32, (tk, LANES), 0).astype(F32)
    key_off = jnp.where(lane < AUG_PIECES, key_off, 0.0).astype(BF16)
    q_off = lax.broadcasted_iota(jnp.int32, (1, tq), 1).astype(F32)

    def tile(ks, scores_of, shift_of):
        k_all = k_ref[pl.ds(ks, tk), :]
        v_all = vt_ref[0, :, pl.ds(ks, tk)]
        scores = [scores_of(k_all[:, (i // 2) * LANES:(i // 2 + 1) * LANES], i)
                  for i in range(2 * n_heads)]
        for i in range(2 * n_heads):
            h = i // 2
            _flash_step_refs(i, scores[i], shift_of(_alibi_slope(h, n_heads) * LOG2E),
                             _with_sum_rows(v_all[h * LANES:(h + 1) * LANES]), m_ref, acc_ref)

    def full_tile(j, carry):
        off = (j * tk - qi * tq).astype(F32) - q_off
        tile(pl.multiple_of(j * tk, tk),
             lambda k2, i: _dot(jnp.concatenate([k2, key_off], axis=1), qm_ref[i]),
             lambda s2: s2 * off)
        return carry

    lax.fori_loop(0, qi * n_diag, full_tile, 0)
    r = lax.broadcasted_iota(jnp.int32, (tk, tq), 0)
    c = lax.broadcasted_iota(jnp.int32, (tk, tq), 1)
    rel = (r - c).astype(F32)
    q_chunk = lax.shift_right_logical(c, CHUNK_SHIFT)
    for d in range(n_diag):
        dist = jnp.abs(rel + float(d * tk))
        visible = lax.shift_right_logical(r + d * tk, CHUNK_SHIFT) <= q_chunk

        def diag_scores(k2, i):
            slope2 = _alibi_slope(i // 2, n_heads) * LOG2E
            return jnp.where(visible, _dot(k2, qm_ref[i, :LANES]) - slope2 * dist, NEG_INF)

        tile(pl.multiple_of(qi * tq + d * tk, tk), diag_scores, lambda s2: 0.0)
    lam = _lambda_scalar(lq1_ref[...], lk1_ref[...], lq2_ref[...], lk2_ref[...], lambda_init)
    for h in range(n_heads):
        blk = slice(h * LANES, (h + 1) * LANES)
        o = (_flash_result(2 * h, LANES, acc_ref) - lam * _flash_result(2 * h + 1, LANES, acc_ref)).T
        o_ref[:, blk] = _head_rms(o, g_ref[:, blk], lambda_init).astype(BF16)


def _diff_prompt(qt, k, vt, lams, g, *, batch, seq, tq, tk, n_heads, lambda_init):
    width = n_heads * 2 * DH_B
    nq = seq // tq
    return pl.pallas_call(
        functools.partial(_diff_prompt_kernel, tq=tq, tk=tk, n_heads=n_heads,
                          lambda_init=lambda_init),
        grid=(batch, nq),
        in_specs=[pl.BlockSpec((1, width, tq), lambda b, i: (b, 0, i)),
                  pl.BlockSpec((seq, width), lambda b, i: (b, 0)),
                  pl.BlockSpec((1, width, seq), lambda b, i: (b, 0, 0))]
                 + [_const_spec(a.shape) for a in lams] + [_const_spec(g.shape)],
        out_specs=pl.BlockSpec((tq, width), lambda b, i: (b * nq + i, 0)),
        out_shape=jax.ShapeDtypeStruct((batch * seq, width), BF16),
        scratch_shapes=_stream_scratch(2 * n_heads, LANES, tq),
        compiler_params=_cparams(2),
        name="diff_prompt",
    )(qt, k, vt, *lams, g)


def _lane_half_masks(rows):
    lane = lax.broadcasted_iota(jnp.int32, (rows, LANES), 1)
    lo = jnp.where(lane < HD_A, 1.0, 0.0)
    return lo, 1.0 - lo


def _masked_q(q2, mask):
    return (q2.astype(F32) * mask).astype(BF16)


def _flash_init(m_ref, l_ref, acc_ref):
    m_ref[...] = jnp.full(m_ref.shape, NEG_INF, F32)
    l_ref[...] = jnp.zeros(l_ref.shape, F32)
    acc_ref[...] = jnp.zeros(acc_ref.shape, F32)


def _flash_step(s, pv, m_ref, l_ref, acc_ref):
    m_prev = m_ref[...]
    m_new = jnp.maximum(m_prev, jnp.max(s, axis=1, keepdims=True))
    alpha = jnp.exp(m_prev - m_new)
    p = jnp.exp(s - m_new)
    l_ref[...] = alpha * l_ref[...] + jnp.sum(p, axis=1, keepdims=True)
    acc_ref[...] = alpha * acc_ref[...] + pv(p.astype(BF16))
    m_ref[...] = m_new


def _fox_decode_kernel(q_ref, kc_ref, vc_ref, kn_ref, vn_ref, ccol_ref, crow_ref, o_ref,
                       m_ref, l_ref, acc_ref, *, tq, tk, n_kv, n_heads):
    j = pl.program_id(1)
    lo, hi = _lane_half_masks(tq)

    @pl.when(j == 0)
    def _():
        _flash_init(m_ref, l_ref, acc_ref)

    def head_loop(scores_of, pv_of, bias_of, mask):
        pending = []
        for pair in range(n_heads // 2):
            blk = slice(pair * LANES, (pair + 1) * LANES)
            q2 = q_ref[:, blk]
            scores = scores_of(blk)
            pv = pv_of(blk)
            for half, hm in enumerate((lo, hi)):
                s = scores(_masked_q(q2, hm)) + bias_of(2 * pair + half)
                if mask is not None:
                    s = jnp.where(mask, s, NEG_INF)
                pending.append((s, pv))
        for h, (s, pv) in enumerate(pending):
            _flash_step(s, pv, m_ref.at[h], l_ref.at[h], acc_ref.at[h])

    @pl.when(j < n_kv)
    def _():
        ks = pl.multiple_of(j * tk, tk)

        def scores_of(blk):
            kt = kc_ref[0, 0, blk, :].astype(BF16)
            return lambda qm: _dot(qm, kt)

        def pv_of(blk):
            vt = vc_ref[0, 0, blk, :].astype(BF16)
            return lambda p: _dot_nt(p, vt)

        head_loop(scores_of, pv_of,
                  lambda h: ccol_ref[0, :, h:h + 1] - crow_ref[0, h:h + 1, pl.ds(ks, tk)], None)

    @pl.when(j == n_kv)
    def _():
        past = n_kv * tk
        q_pos = lax.broadcasted_iota(jnp.int32, (tq, tq), 0)
        k_pos = lax.broadcasted_iota(jnp.int32, (tq, tq), 1)
        head_loop(lambda blk: (lambda qm: _dot_nt(qm, kn_ref[:, blk])),
                  lambda blk: (lambda p: _dot(p, vn_ref[:, blk])),
                  lambda h: ccol_ref[0, :, h:h + 1] - crow_ref[0, h:h + 1, past:past + tq],
                  k_pos <= q_pos)
        for pair in range(n_heads // 2):
            blk = slice(pair * LANES, (pair + 1) * LANES)
            o0 = acc_ref[2 * pair] / l_ref[2 * pair]
            o1 = acc_ref[2 * pair + 1] / l_ref[2 * pair + 1]
            o_ref[:, blk] = (o0 * lo + o1 * hi).astype(BF16)


def _fox_decode(q, kt_cache, vt_cache, k_new, v_new, c_col, c_row, *, layer, tq, tk, n_heads):
    _, batch, width, past = kt_cache.shape
    n_kv = past // tk
    new_blk = pl.BlockSpec((tq, width), lambda b, j: (b, 0))
    cache_blk = pl.BlockSpec((1, 1, width, tk),
                             lambda b, j: (layer, b, 0, jnp.minimum(j, n_kv - 1)))
    return pl.pallas_call(
        functools.partial(_fox_decode_kernel, tq=tq, tk=tk, n_kv=n_kv, n_heads=n_heads),
        grid=(batch, n_kv + 1),
        in_specs=[new_blk, cache_blk, cache_blk, new_blk, new_blk,
                  pl.BlockSpec((1, tq, n_heads), lambda b, j: (b, past // tq, 0)),
                  pl.BlockSpec((1, n_heads, c_row.shape[2]), lambda b, j: (b, 0, 0))],
        out_specs=new_blk,
        out_shape=jax.ShapeDtypeStruct((batch * tq, width), BF16),
        scratch_shapes=[pltpu.VMEM((n_heads, tq, 1), F32), pltpu.VMEM((n_heads, tq, 1), F32),
                        pltpu.VMEM((n_heads, tq, LANES), F32)],
        compiler_params=_cparams(2),
        name="fox_decode",
    )(q, kt_cache, vt_cache, k_new, v_new, c_col, c_row)


def _diff_decode_kernel(q_ref, kc_ref, vc_ref, kn_ref, vn_ref, lq1_ref, lk1_ref, lq2_ref, lk2_ref,
                        g_ref, o_ref, m_ref, l_ref, acc_ref,
                        *, tq, tk, n_kv, n_heads, lambda_init):
    j = pl.program_id(1)
    lo, hi = _lane_half_masks(tq)
    past = n_kv * tk
    q_abs = (past + lax.broadcasted_iota(jnp.int32, (tq, 1), 0)).astype(F32)

    @pl.when(j == 0)
    def _():
        _flash_init(m_ref, l_ref, acc_ref)

    def head_loop(scores_of, pv_of, dist):
        pending = []
        for h in range(n_heads):
            blk = slice(h * LANES, (h + 1) * LANES)
            slope = _alibi_slope(h, n_heads)
            q2 = q_ref[:, blk]
            scores = scores_of(blk)
            pv = pv_of(h)
            for hm in (lo, hi):
                pending.append((scores(_masked_q(q2, hm)) - slope * dist, pv))
        for idx, (s, pv) in enumerate(pending):
            _flash_step(s, pv, m_ref.at[idx], l_ref.at[idx], acc_ref.at[idx])

    @pl.when(j < n_kv)
    def _():
        k_abs = (j * tk + lax.broadcasted_iota(jnp.int32, (1, tk), 1)).astype(F32)

        def scores_of(blk):
            kt = kc_ref[0, 0, blk, :].astype(BF16)
            return lambda qm: _dot(qm, kt)

        def pv_of(h):
            v2 = vc_ref[0, 0, pl.ds(h, tk, stride=n_heads), :].astype(BF16)
            return lambda p: _dot(p, v2)

        head_loop(scores_of, pv_of, jnp.abs(q_abs - k_abs))

    @pl.when(j == n_kv)
    def _():
        assert past % CHUNK == 0 and tq <= CHUNK
        k_abs = (past + lax.broadcasted_iota(jnp.int32, (1, tq), 1)).astype(F32)
        head_loop(lambda blk: (lambda qm: _dot_nt(qm, kn_ref[:, blk])),
                  lambda h: (lambda p: _dot(p, vn_ref[:, h * LANES:(h + 1) * LANES])),
                  jnp.abs(q_abs - k_abs))
        lam = _lambda_scalar(lq1_ref[...], lk1_ref[...], lq2_ref[...], lk2_ref[...], lambda_init)
        for h in range(n_heads):
            blk = slice(h * LANES, (h + 1) * LANES)
            o = acc_ref[2 * h] / l_ref[2 * h] - lam * (acc_ref[2 * h + 1] / l_ref[2 * h + 1])
            o_ref[:, blk] = _head_rms(o, g_ref[:, blk], lambda_init).astype(BF16)


def _diff_decode(q, kt_cache, v_cache, k_new, v_new, lams, g, *, layer, tq, tk, n_heads,
                 lambda_init):
    _, batch, width, past = kt_cache.shape
    n_kv = past // tk
    new_blk = pl.BlockSpec((tq, width), lambda b, j: (b, 0))
    return pl.pallas_call(
        functools.partial(_diff_decode_kernel, tq=tq, tk=tk, n_kv=n_kv, n_heads=n_heads,
                          lambda_init=lambda_init),
        grid=(batch, n_kv + 1),
        in_specs=[new_blk,
                  pl.BlockSpec((1, 1, width, tk),
                               lambda b, j: (layer, b, 0, jnp.minimum(j, n_kv - 1))),
                  pl.BlockSpec((1, 1, tk * n_heads, LANES),
                               lambda b, j: (layer, b, jnp.minimum(j, n_kv - 1), 0)),
                  new_blk, new_blk]
                 + [_const_spec(a.shape) for a in lams] + [_const_spec(g.shape)],
        out_specs=new_blk,
        out_shape=jax.ShapeDtypeStruct((batch * tq, width), BF16),
        scratch_shapes=[pltpu.VMEM((2 * n_heads, tq, 1), F32),
                        pltpu.VMEM((2 * n_heads, tq, 1), F32),
                        pltpu.VMEM((2 * n_heads, tq, LANES), F32)],
        compiler_params=_cparams(2),
        name="diff_decode",
    )(q, kt_cache, v_cache, k_new, v_new, *lams, g)


def _post_kernel(x_ref, oa_ref, ob_ref, wgt_ref, wba_ref, wbb_ref, wo_ref, g_ref, b_ref, o_ref,
                 *, alpha):
    x = x_ref[...]
    xb = x.astype(BF16)
    d = x.shape[1]
    merged = _sigmoid(_dot_nt(xb, wgt_ref[:d, :])) * _dot(oa_ref[...], wba_ref[...])
    merged = merged + _sigmoid(_dot_nt(xb, wgt_ref[d:, :])) * _dot(ob_ref[...], wbb_ref[...])
    y = alpha * x + _dot(merged.astype(BF16), wo_ref[...])
    o_ref[...] = _layer_norm(y, g_ref[...], b_ref[...])


def _post(x, oa, ob, wt_gate, w_ba, w_bb, w_o, g, b, *, layer, tm, alpha):
    m, d = x.shape
    row = lambda i: (i, 0)
    return pl.pallas_call(
        functools.partial(_post_kernel, alpha=alpha),
        grid=(m // tm,),
        in_specs=[pl.BlockSpec((tm, d), row), pl.BlockSpec((tm, oa.shape[1]), row),
                  pl.BlockSpec((tm, ob.shape[1]), row), _const_spec(wt_gate.shape)]
                 + [_layer_spec(a.shape, layer) for a in (w_ba, w_bb, w_o, g, b)],
        out_specs=pl.BlockSpec((tm, d), row),
        out_shape=jax.ShapeDtypeStruct((m, d), F32),
        compiler_params=_cparams(1),
        name="merge_outproj_ln",
    )(x, oa, ob, wt_gate, w_ba, w_bb, w_o, g, b)


def _ffn_kernel(x_ref, p_ref, wg_ref, wu_ref, wd_ref, wpg_ref, wpp_ref, g_ref, b_ref, o_ref,
                *, alpha):
    x = x_ref[...]
    xb = x.astype(BF16)
    hg = _dot(xb, wg_ref[...])
    hidden = (hg * _sigmoid(hg) * _dot(xb, wu_ref[...])).astype(BF16)
    y = alpha * x + _dot(hidden, wd_ref[...])
    ple = _sigmoid(_dot(xb, wpg_ref[...])) * _dot(p_ref[...].astype(BF16), wpp_ref[...])
    o_ref[...] = _layer_norm(y + ple, g_ref[...], b_ref[...])


def _ffn(x, p, w_g, w_u, w_d, w_pg, w_pp, g, b, *, layer, tm, alpha):
    m, d = x.shape
    row = lambda i: (i, 0)
    return pl.pallas_call(
        functools.partial(_ffn_kernel, alpha=alpha),
        grid=(m // tm,),
        in_specs=[pl.BlockSpec((tm, d), row),
                  pl.BlockSpec((None, tm, p.shape[2]), lambda i: (layer, i, 0))]
                 + [_layer_spec(a.shape, layer) for a in (w_g, w_u, w_d, w_pg, w_pp, g, b)],
        out_specs=pl.BlockSpec((tm, d), row),
        out_shape=jax.ShapeDtypeStruct((m, d), F32),
        compiler_params=_cparams(1),
        name="swiglu_ple_ln",
    )(x, p, w_g, w_u, w_d, w_pg, w_pp, g, b)


def _row_tile(m):
    for tm in (512, 256, 128):
        if m % tm == 0:
            return tm
    raise ValueError(f"row count {m} is not a multiple of 128")


def kernel(x_prompt, x_sample, p_prompt, p_sample, cache_fox_k, cache_fox_v, cache_fox_logf, cache_diff_k, cache_diff_v, w_in, b_forget, lambda_q1, lambda_k1, lambda_q2, lambda_k2, diff_norm_g, w_branch_fox, w_branch_diff, w_out, ln1_g, ln1_b, w_ffn_gate, w_ffn_up, w_ffn_down, w_ple_gate, w_ple_proj, ln2_g, ln2_b):
    batch, seq, d_model = x_prompt.shape
    dec_batch, dec_seq, _ = x_sample.shape
    depth = w_in.shape[0]
    past = cache_fox_k.shape[2]
    h_a = cache_fox_k.shape[3]
    h_b = cache_diff_k.shape[3]
    w_a = h_a * HD_A
    w_b = h_b * 2 * DH_B
    assert w_a == w_b and w_in.shape[2] == 3 * w_a + h_a + 3 * w_b + 2 * d_model
    assert HD_A == DH_B
    alpha = (2 * depth) ** 0.25
    m_p = batch * seq
    m_s = dec_batch * dec_seq
    tm_p = _row_tile(seq)
    tm_s = _row_tile(m_s)
    tq, tk = min(256, seq), min(256, seq)
    tk_dec = min(1024, past)
    assert seq % tq == 0 and tq % tk == 0 and tk % CHUNK == 0 and past % tk_dec == 0
    t_dec = past + dec_seq
    t_dec_pad = -(-t_dec // SCAN_BLOCK) * SCAN_BLOCK
    qk_scale = HD_A ** -0.5
    f_pad = 16

    xp = x_prompt.reshape(m_p, d_model)
    xs = x_sample.reshape(m_s, d_model)

    wt_in = jnp.swapaxes(w_in, 1, 2)
    kt_cache_a = jnp.transpose(cache_fox_k, (0, 1, 3, 4, 2)).reshape(depth, dec_batch, w_a, past)
    vt_cache_a = jnp.transpose(cache_fox_v, (0, 1, 3, 4, 2)).reshape(depth, dec_batch, w_a, past)
    kt_cache_b = jnp.transpose(cache_diff_k, (0, 1, 3, 4, 5, 2)).reshape(depth, dec_batch, w_b, past)
    lf_cache = jnp.swapaxes(cache_fox_logf, 2, 3).astype(F32)
    v_cache_b = cache_diff_v.reshape(depth, dec_batch, past * h_b, 2 * DH_B)

    post_w = (w_branch_fox.astype(BF16), w_branch_diff.astype(BF16), w_out.astype(BF16),
              ln1_g.reshape(depth, 1, d_model), ln1_b.reshape(depth, 1, d_model))
    ffn_w = (w_ffn_gate.astype(BF16), w_ffn_up.astype(BF16), w_ffn_down.astype(BF16),
             w_ple_gate.astype(BF16), w_ple_proj.astype(BF16),
             ln2_g.reshape(depth, 1, d_model), ln2_b.reshape(depth, 1, d_model))
    pp = p_prompt.reshape(depth, m_p, -1)
    ps = p_sample.reshape(depth, m_s, -1)

    o_f = 3 * w_a
    o_qb = o_f + h_a
    o_ga = o_qb + 3 * w_b

    stacked = None
    new_s = [[] for _ in range(5)]
    for l in range(depth):
        lambda_init = 0.8 - 0.6 * math.exp(-0.3 * l)
        wt = wt_in[l]
        wt_qkv = jnp.concatenate([wt[:o_f], wt[o_qb:o_ga]], axis=0).astype(BF16)
        wt_f = jnp.pad(wt[o_f:o_qb], ((0, f_pad - h_a), (0, 0))).astype(BF16)
        b_f = jnp.pad(b_forget[l], (0, f_pad - h_a))
        wt_gate = wt[o_ga:].astype(BF16)

        (qat, ka, kat, vat, vat_b, qbt, kb, kbt, vb, vbt_b, logf_p) = _inproj_prompt(
            xp, wt_qkv, wt_f, b_f.reshape(f_pad, 1), stacked, layer=l, depth=depth, batch=batch,
            seq=seq, tm=tm_p, n_heads=h_a, q_scale=qk_scale * LOG2E)
        stacked = (kat, vat, kbt, vb, logf_p)
        (qa_s, qb_s, ka_s, va_s, kb_s, vb_s, ka_sb, va_sb, kb_sb, vb_sb, logf_s) = _inproj_decode(
            xs, wt_qkv, wt_f, b_f.reshape(1, f_pad), n_heads=h_a, q_scale=qk_scale)

        c_row_p, aug_p = _cumsum_aug(logf_p[l], LOG2E, width=w_a)
        lf_s = jnp.swapaxes(logf_s.reshape(dec_batch, dec_seq, h_a), 1, 2)
        lf_all = jnp.concatenate([lf_cache[l], lf_s], axis=2)
        lf_all = jnp.pad(lf_all, ((0, 0), (0, 0), (0, t_dec_pad - t_dec)))
        c_row_s = _cumsum_rows(lf_all.reshape(dec_batch * h_a, t_dec_pad), 1.0)
        c_row_s = c_row_s.reshape(dec_batch, h_a, t_dec_pad)
        c_col_s = jnp.swapaxes(c_row_s, 1, 2)

        lams = [a[l].reshape(1, DH_B) for a in (lambda_q1, lambda_k1, lambda_q2, lambda_k2)]
        g_diff = diff_norm_g[l].reshape(1, w_b)

        oa_p = _fox_prompt(qat, ka, vat_b, c_row_p, aug_p,
                           batch=batch, seq=seq, tq=tq, tk=tk, n_heads=h_a)
        ob_p = _diff_prompt(qbt, kb, vbt_b, lams, g_diff, batch=batch, seq=seq, tq=tq, tk=tk,
                            n_heads=h_b, lambda_init=lambda_init)
        oa_s = _fox_decode(qa_s, kt_cache_a, vt_cache_a, ka_sb, va_sb, c_col_s, c_row_s,
                           layer=l, tq=dec_seq, tk=tk_dec, n_heads=h_a)
        ob_s = _diff_decode(qb_s, kt_cache_b, v_cache_b, kb_sb, vb_sb, lams, g_diff,
                            layer=l, tq=dec_seq, tk=tk_dec, n_heads=h_b, lambda_init=lambda_init)

        xp = _post(xp, oa_p, ob_p, wt_gate, *post_w, layer=l, tm=tm_p, alpha=alpha)
        xp = _ffn(xp, pp, *ffn_w, layer=l, tm=tm_p, alpha=alpha)
        xs = _post(xs, oa_s, ob_s, wt_gate, *post_w, layer=l, tm=tm_s, alpha=alpha)
        xs = _ffn(xs, ps, *ffn_w, layer=l, tm=tm_s, alpha=alpha)

        for lst, r in zip(new_s, (ka_s, va_s, logf_s, kb_s, vb_s)):
            lst.append(r)

    kat, vat, kbt, vb, logf_p = stacked
    ka_s, va_s, logf_s, kb_s, vb_s = [jnp.stack(a) for a in new_s]
    fox_k_p = jnp.transpose(kat.reshape(depth, batch, h_a, HD_A, seq), (0, 1, 4, 2, 3))
    fox_v_p = jnp.transpose(vat.reshape(depth, batch, h_a, HD_A, seq), (0, 1, 4, 2, 3))
    fox_lf_p = jnp.swapaxes(logf_p, 2, 3)
    diff_k_p = jnp.transpose(kbt.reshape(depth, batch, h_b, 2, DH_B, seq), (0, 1, 5, 2, 3, 4))
    return (xp.reshape(batch, seq, d_model), xs.reshape(dec_batch, dec_seq, d_model),
            fox_k_p, fox_v_p, fox_lf_p, diff_k_p, vb,
            ka_s.reshape(depth, dec_batch, dec_seq, h_a, HD_A),
            va_s.reshape(depth, dec_batch, dec_seq, h_a, HD_A),
            logf_s.reshape(depth, dec_batch, dec_seq, h_a),
            kb_s.reshape(depth, dec_batch, dec_seq, h_b, 2, DH_B),
            vb_s.reshape(depth, dec_batch, dec_seq, h_b, 2 * DH_B))
```

```python
import functools
import math

import jax
import jax.numpy as jnp
import numpy as np
from jax import lax
from jax.experimental import pallas as pl
from jax.experimental.pallas import tpu as pltpu

F32 = jnp.float32
BF16 = jnp.bfloat16

HD_A = 64
DH_B = 64
CHUNK = 64
CHUNK_SHIFT = 6
LN_EPS = 1e-5
RMS_EPS = 1e-5
NEG_INF = -1e30
LOG2E = math.log2(math.e)
LANES = 128
VMEM_LIMIT = 56 * 1024 * 1024


def _cparams(n_axes):
    return pltpu.CompilerParams(dimension_semantics=("arbitrary",) * n_axes,
                                vmem_limit_bytes=VMEM_LIMIT)


def _const_spec(shape):
    zeros = (0,) * len(shape)
    return pl.BlockSpec(shape, lambda *_: zeros, pipeline_mode=pl.Buffered(1))


def _layer_spec(shape, layer):
    zeros = (0,) * (len(shape) - 1)
    return pl.BlockSpec((None,) + tuple(shape[1:]), lambda *_: (layer,) + zeros,
                        pipeline_mode=pl.Buffered(1))


def _dot(a, b):
    return jnp.dot(a, b, preferred_element_type=F32)


def _dot_nt(a, b):
    return lax.dot_general(a, b, (((1,), (1,)), ((), ())), preferred_element_type=F32)


def _sigmoid(x):
    return 1.0 / (1.0 + jnp.exp(-x))


def _log_sigmoid(z):
    return jnp.minimum(z, 0.0) - jnp.log1p(jnp.exp(-jnp.abs(z)))


def _layer_norm(y, g, b):
    mu = jnp.mean(y, axis=-1, keepdims=True)
    yc = y - mu
    var = jnp.mean(yc * yc, axis=-1, keepdims=True)
    return yc * lax.rsqrt(var + LN_EPS) * g + b


def _inproj_prompt_kernel(x_ref, wt_ref, wft_ref, bf_ref, *refs, width, n_heads, q_scale):
    (qat_ref, ka_ref, kat_ref, vat_ref, vatb_ref,
     qbt_ref, kb_ref, kbt_ref, vb_ref, vbtb_ref, logf_ref) = refs[-11:]
    xb = x_ref[...].astype(BF16)

    def proj_t(i):
        return _dot_nt(wt_ref[i * width:(i + 1) * width, :], xb)

    qat_ref[0] = (proj_t(0) * q_scale).astype(BF16)
    kat = proj_t(1)
    kat_ref[0, 0] = kat
    ka_ref[...] = kat.T.astype(BF16)
    vat = proj_t(2)
    vat_ref[0, 0] = vat
    vatb_ref[0] = vat.astype(BF16)
    qbt_ref[0] = (proj_t(3) * q_scale).astype(BF16)
    kbt = proj_t(4)
    kbt_ref[0, 0] = kbt
    kb_ref[...] = kbt.T.astype(BF16)
    vbt = proj_t(5)
    vbtb_ref[0] = vbt.astype(BF16)
    vb = vbt.T
    for h in range(width // LANES):
        vb_ref[0, 0, :, h, :] = vb[:, h * LANES:(h + 1) * LANES]
    z = _dot_nt(wft_ref[...], xb) + bf_ref[...]
    logf_ref[0, 0] = _log_sigmoid(z)[:n_heads, :]


def _inproj_prompt(x, wt_qkv, wt_f, b_f_col, stacked, *, layer, depth, batch, seq, tm, n_heads,
                   q_scale):
    m, d = x.shape
    width = wt_qkv.shape[0] // 6
    nt = seq // tm
    row = lambda i: (i, 0)
    tr = lambda i: (i // nt, 0, i % nt)
    tr_l = lambda i: (layer, i // nt, 0, i % nt)
    t_blk = pl.BlockSpec((1, width, tm), tr)
    t_blk_l = pl.BlockSpec((1, 1, width, tm), tr_l)
    n_blk = pl.BlockSpec((tm, width), row)
    t_f32 = jax.ShapeDtypeStruct((depth, batch, width, seq), F32)
    t_b16 = jax.ShapeDtypeStruct((batch, width, seq), BF16)
    n_b16 = jax.ShapeDtypeStruct((m, width), BF16)
    vb_heads = width // LANES
    out_specs = [t_blk, n_blk, t_blk_l, t_blk_l, t_blk, t_blk, n_blk, t_blk_l,
                 pl.BlockSpec((1, 1, tm, vb_heads, LANES),
                              lambda i: (layer, i // nt, i % nt, 0, 0)),
                 t_blk, pl.BlockSpec((1, 1, n_heads, tm), tr_l)]
    out_shape = [t_b16, n_b16, t_f32, t_f32, t_b16, t_b16, n_b16, t_f32,
                 jax.ShapeDtypeStruct((depth, batch, seq, vb_heads, LANES), F32),
                 t_b16, jax.ShapeDtypeStruct((depth, batch, n_heads, seq), F32)]
    stacked_outputs = (2, 3, 7, 8, 10)
    in_specs = [pl.BlockSpec((tm, d), row), _const_spec(wt_qkv.shape), _const_spec(wt_f.shape),
                _const_spec(b_f_col.shape)]
    args = [x, wt_qkv, wt_f, b_f_col]
    aliases = {}
    if stacked is not None:
        aliases = {len(args) + i: o for i, o in enumerate(stacked_outputs)}
        in_specs += [pl.BlockSpec(memory_space=pl.ANY)] * len(stacked)
        args += list(stacked)
    return pl.pallas_call(
        functools.partial(_inproj_prompt_kernel, width=width, n_heads=n_heads, q_scale=q_scale),
        grid=(m // tm,),
        in_specs=in_specs,
        out_specs=out_specs,
        out_shape=out_shape,
        input_output_aliases=aliases,
        compiler_params=_cparams(1),
        name="inproj_prompt",
    )(*args)


def _inproj_decode_kernel(x_ref, wt_ref, wft_ref, bf_ref,
                          qa_ref, qb_ref, ka_ref, va_ref, kb_ref, vb_ref,
                          kab_ref, vab_ref, kbb_ref, vbb_ref, logf_ref, *, width, n_heads, q_scale):
    xb = x_ref[...].astype(BF16)

    def proj(i):
        return _dot_nt(xb, wt_ref[i * width:(i + 1) * width, :])

    qa_ref[...] = (proj(0) * q_scale).astype(BF16)
    qb_ref[...] = (proj(3) * q_scale).astype(BF16)
    for i, (full_ref, half_ref) in ((1, (ka_ref, kab_ref)), (2, (va_ref, vab_ref)),
                                    (4, (kb_ref, kbb_ref)), (5, (vb_ref, vbb_ref))):
        val = proj(i)
        full_ref[...] = val
        half_ref[...] = val.astype(BF16)
    z = _dot_nt(xb, wft_ref[...]) + bf_ref[...]
    logf_ref[...] = _log_sigmoid(z)[:, :n_heads]


def _inproj_decode(x, wt_qkv, wt_f, b_f_row, *, n_heads, q_scale):
    m, d = x.shape
    width = wt_qkv.shape[0] // 6
    out_shape = ([jax.ShapeDtypeStruct((m, width), BF16)] * 2
                 + [jax.ShapeDtypeStruct((m, width), F32)] * 4
                 + [jax.ShapeDtypeStruct((m, width), BF16)] * 4
                 + [jax.ShapeDtypeStruct((m, n_heads), F32)])
    return pl.pallas_call(
        functools.partial(_inproj_decode_kernel, width=width, n_heads=n_heads, q_scale=q_scale),
        out_shape=out_shape,
        compiler_params=pltpu.CompilerParams(vmem_limit_bytes=VMEM_LIMIT),
        name="inproj_decode",
    )(x, wt_qkv, wt_f, b_f_row)


SCAN_BLOCK = 256


def _split3(a):
    a1 = a.astype(BF16)
    r1 = a - a1.astype(F32)
    a2 = r1.astype(BF16)
    a3 = (r1 - a2.astype(F32)).astype(BF16)
    return a1, a2, a3


def _split3_const(x):
    pieces, rest = [], np.float32(x)
    for _ in range(3):
        p = np.float32(np.asarray(rest, dtype=jnp.bfloat16))
        pieces.append(float(p))
        rest = np.float32(rest - p)
    return pieces


def _cumsum_kernel(lf_ref, c_ref, *, n_blk, scale):
    rows = lf_ref.shape[0]
    r = lax.broadcasted_iota(jnp.int32, (SCAN_BLOCK, SCAN_BLOCK), 0)
    c = lax.broadcasted_iota(jnp.int32, (SCAN_BLOCK, SCAN_BLOCK), 1)
    upper = jnp.where(r <= c, 1.0, 0.0).astype(BF16)
    carry = jnp.zeros((rows, 1), F32)
    for j in range(n_blk):
        sl = slice(j * SCAN_BLOCK, (j + 1) * SCAN_BLOCK)
        a1, a2, a3 = _split3(lf_ref[:, sl])
        blk = _dot(a1, upper) + _dot(a2, upper) + _dot(a3, upper) + carry
        c_ref[:, sl] = blk * scale
        carry = blk[:, SCAN_BLOCK - 1:SCAN_BLOCK]


def _cumsum_rows(lf_rows, scale):
    rows, t = lf_rows.shape
    assert t % SCAN_BLOCK == 0
    return pl.pallas_call(
        functools.partial(_cumsum_kernel, n_blk=t // SCAN_BLOCK, scale=scale),
        out_shape=jax.ShapeDtypeStruct((rows, t), F32),
        compiler_params=pltpu.CompilerParams(vmem_limit_bytes=VMEM_LIMIT),
        name="logf_cumsum",
    )(lf_rows)


AUG_PIECES = 3


def _aug_lane(h, piece):
    return AUG_PIECES * (h % 2) + piece


def _cumsum_aug_kernel(lf_rows_ref, lf_cols_ref, place_ref, c_ref, aug_ref, *, n_blk, scale):
    rows = lf_rows_ref.shape[0]
    batch, _, width = aug_ref.shape
    r = lax.broadcasted_iota(jnp.int32, (SCAN_BLOCK, SCAN_BLOCK), 0)
    c = lax.broadcasted_iota(jnp.int32, (SCAN_BLOCK, SCAN_BLOCK), 1)
    upper = jnp.where(r <= c, 1.0, 0.0).astype(BF16)
    lower = jnp.where(c <= r, 1.0, 0.0).astype(BF16)
    carry_r = jnp.zeros((rows, 1), F32)
    carry_c = jnp.zeros((1, rows), F32)
    for j in range(n_blk):
        sl = slice(j * SCAN_BLOCK, (j + 1) * SCAN_BLOCK)
        a1, a2, a3 = _split3(lf_rows_ref[:, sl])
        blk_r = _dot(a1, upper) + _dot(a2, upper) + _dot(a3, upper) + carry_r
        c_ref[:, sl] = blk_r * scale
        carry_r = blk_r[:, SCAN_BLOCK - 1:SCAN_BLOCK]
        b1, b2, b3 = _split3(lf_cols_ref[sl, :])
        blk_c = _dot(lower, b1) + _dot(lower, b2) + _dot(lower, b3) + carry_c
        carry_c = blk_c[SCAN_BLOCK - 1:SCAN_BLOCK, :]
        pieces = jnp.concatenate(_split3(blk_c * scale), axis=1)
        placed = _dot(pieces, place_ref[...]).astype(BF16)
        for b in range(batch):
            aug_ref[b, sl, :] = placed[:, b * width:(b + 1) * width]


def _placement_matrix(batch, n_heads, width):
    rows = batch * n_heads
    place = np.zeros((AUG_PIECES * rows, batch * width), np.float32)
    for b in range(batch):
        for h in range(n_heads):
            for p in range(AUG_PIECES):
                place[p * rows + b * n_heads + h,
                      b * width + (h // 2) * LANES + _aug_lane(h, p)] = 1.0
    return jnp.asarray(place, BF16)


def _cumsum_aug(lf, scale, *, width):
    batch, n_heads, t = lf.shape
    assert t % SCAN_BLOCK == 0
    rows = batch * n_heads
    c_rows, aug = pl.pallas_call(
        functools.partial(_cumsum_aug_kernel, n_blk=t // SCAN_BLOCK, scale=scale),
        out_shape=[jax.ShapeDtypeStruct((rows, t), F32),
                   jax.ShapeDtypeStruct((batch, t, width), BF16)],
        compiler_params=pltpu.CompilerParams(vmem_limit_bytes=VMEM_LIMIT),
        name="logf_cumsum_aug",
    )(lf.reshape(rows, t), jnp.transpose(lf, (2, 0, 1)).reshape(t, rows),
      _placement_matrix(batch, n_heads, width))
    return c_rows.reshape(batch, n_heads, t), aug


def _lambda_scalar(lq1, lk1, lq2, lk2, lambda_init):
    return (jnp.exp(jnp.sum(lq1 * lk1, axis=1, keepdims=True))
            - jnp.exp(jnp.sum(lq2 * lk2, axis=1, keepdims=True)) + lambda_init)


def _head_rms(o, g_row, lambda_init):
    o = o * lax.rsqrt(jnp.mean(o * o, axis=-1, keepdims=True) + RMS_EPS)
    return o * g_row * (1.0 - lambda_init)


def _alibi_slope(h, n_heads):
    return 2.0 ** (-8.0 * (h + 1) / n_heads)


def _sublane_half_masks(cols):
    row = lax.broadcasted_iota(jnp.int32, (LANES, cols), 0)
    lo = jnp.where(row < HD_A, 1.0, 0.0)
    return lo, 1.0 - lo


SUM_ROWS = 16


def _with_sum_rows(v_t):
    return jnp.concatenate([v_t, jnp.ones((SUM_ROWS, v_t.shape[1]), BF16)], axis=0)


def _flash_step_refs(i, t, shift, v_aug, m_ref, acc_ref, qs=slice(None)):
    m = m_ref[i, :, qs]
    m_new = jnp.maximum(m, jnp.max(t, axis=0, keepdims=True) + shift)
    p = jnp.exp2(t - (m_new - shift)).astype(BF16)
    acc_ref[i, :, qs] = jnp.exp2(m - m_new) * acc_ref[i, :, qs] + _dot(v_aug, p)
    m_ref[i, :, qs] = m_new


def _flash_init_refs(m_ref, acc_ref):
    m_ref[...] = jnp.full(m_ref.shape, NEG_INF, F32)
    acc_ref[...] = jnp.zeros(acc_ref.shape, F32)


def _flash_result(i, d, acc_ref):
    acc = acc_ref[i]
    return acc[:d] / acc[d:d + 1]


def _stream_scratch(n_streams, d, tq):
    return [pltpu.VMEM((n_streams, 2 * LANES, tq), BF16), pltpu.VMEM((n_streams, 1, tq), F32),
            pltpu.VMEM((n_streams, d + SUM_ROWS, tq), F32)]


def _fox_prompt_kernel(qt_ref, k_ref, vt_ref, crow_ref, aug_ref, o_ref, qm_ref, m_ref, acc_ref,
                       *, tq, tk, n_heads):
    qi = pl.program_id(1)
    n_diag = tq // tk
    lo, hi = _sublane_half_masks(tq)
    row = lax.broadcasted_iota(jnp.int32, (LANES, tq), 0)
    for h in range(n_heads):
        blk = slice((h // 2) * LANES, (h // 2 + 1) * LANES)
        qm_ref[h, :LANES] = (qt_ref[0, blk, :].astype(F32) * (lo, hi)[h % 2]).astype(BF16)
        picks = (row >= _aug_lane(h, 0)) & (row < _aug_lane(h, AUG_PIECES))
        qm_ref[h, LANES:] = jnp.where(picks, -1.0, 0.0).astype(BF16)
    _flash_init_refs(m_ref, acc_ref)

    def tile(ks, mask, qs=slice(None)):
        v_all = vt_ref[0, :, pl.ds(ks, tk)]
        scores = []
        for h in range(n_heads):
            blk = slice((h // 2) * LANES, (h // 2 + 1) * LANES)
            k_aug = jnp.concatenate([k_ref[pl.ds(ks, tk), blk], aug_ref[0, pl.ds(ks, tk), blk]],
                                    axis=1)
            t = _dot(k_aug, qm_ref[h, :, qs])
            if mask is not None:
                t = jnp.where(mask, t, NEG_INF)
            scores.append(t)
        for h in range(n_heads):
            _flash_step_refs(h, scores[h], crow_ref[0, h:h + 1, qs],
                             _with_sum_rows(v_all[h * HD_A:(h + 1) * HD_A]), m_ref, acc_ref, qs)

    def full_tile(j, carry):
        tile(pl.multiple_of(j * tk, tk), None)
        return carry

    lax.fori_loop(0, qi * n_diag, full_tile, 0)
    for d in range(n_diag):
        nq = tq - d * tk
        r = lax.broadcasted_iota(jnp.int32, (tk, nq), 0)
        c = lax.broadcasted_iota(jnp.int32, (tk, nq), 1)
        tile(pl.multiple_of(qi * tq + d * tk, tk), r <= c, slice(d * tk, tq))
    for pair in range(n_heads // 2):
        blk = slice(pair * LANES, (pair + 1) * LANES)
        o_t = jnp.concatenate([_flash_result(2 * pair, HD_A, acc_ref),
                               _flash_result(2 * pair + 1, HD_A, acc_ref)], axis=0)
        o_ref[:, blk] = o_t.T.astype(BF16)


def _fox_prompt(qt, k, vt, c_row, aug, *, batch, seq, tq, tk, n_heads):
    width = n_heads * HD_A
    nq = seq // tq
    return pl.pallas_call(
        functools.partial(_fox_prompt_kernel, tq=tq, tk=tk, n_heads=n_heads),
        grid=(batch, nq),
        in_specs=[pl.BlockSpec((1, width, tq), lambda b, i: (b, 0, i)),
                  pl.BlockSpec((seq, width), lambda b, i: (b, 0)),
                  pl.BlockSpec((1, width, seq), lambda b, i: (b, 0, 0)),
                  pl.BlockSpec((1, n_heads, tq), lambda b, i: (b, 0, i)),
                  pl.BlockSpec((1, seq, width), lambda b, i: (b, 0, 0))],
        out_specs=pl.BlockSpec((tq, width), lambda b, i: (b * nq + i, 0)),
        out_shape=jax.ShapeDtypeStruct((batch * seq, width), BF16),
        scratch_shapes=_stream_scratch(n_heads, HD_A, tq),
        compiler_params=_cparams(2),
        name="fox_prompt",
    )(qt, k, vt, c_row, aug)


def _diff_prompt_kernel(qt_ref, k_ref, vt_ref, lq1_ref, lk1_ref, lq2_ref, lk2_ref, g_ref, o_ref,
                        qm_ref, m_ref, acc_ref, *, tq, tk, n_heads, lambda_init):
    qi = pl.program_id(1)
    n_diag = tq // tk
    lo, hi = _sublane_half_masks(tq)
    row = lax.broadcasted_iota(jnp.int32, (LANES, tq), 0)
    for i in range(2 * n_heads):
        blk = slice((i // 2) * LANES, (i // 2 + 1) * LANES)
        qm_ref[i, :LANES] = (qt_ref[0, blk, :].astype(F32) * (lo, hi)[i % 2]).astype(BF16)
        s1, s2, s3 = _split3_const(_alibi_slope(i // 2, n_heads) * LOG2E)
        slope_rows = jnp.where(row == 0, s1, jnp.where(row == 1, s2, jnp.where(row == 2, s3, 0.0)))
        qm_ref[i, LANES:] = slope_rows.astype(BF16)
    _flash_init_refs(m_ref, acc_ref)
    assert tk <= 256
    lane = lax.broadcasted_iota(jnp.int32, (tk, LANES), 1)
    key_off = lax.broadcasted_iota(jnp.int32, (tk, LANES), 0).astype(F32)
    key_off = jnp.where(lane < AUG_PIECES, key_off, 0.0).astype(BF16)
    q_off = lax.broadcasted_iota(jnp.int32, (1, tq), 1).astype(F32)

    def tile(ks, scores_of, shift_of, qs=slice(None)):
        k_all = k_ref[pl.ds(ks, tk), :]
        v_all = vt_ref[0, :, pl.ds(ks, tk)]
        scores = [scores_of(k_all[:, (i // 2) * LANES:(i // 2 + 1) * LANES], i)
                  for i in range(2 * n_heads)]
        for i in range(2 * n_heads):
            h = i // 2
            _flash_step_refs(i, scores[i], shift_of(_alibi_slope(h, n_heads) * LOG2E),
                             _with_sum_rows(v_all[h * LANES:(h + 1) * LANES]), m_ref, acc_ref, qs)

    def full_tile(j, carry):
        off = (j * tk - qi * tq).astype(F32) - q_off
        tile(pl.multiple_of(j * tk, tk),
             lambda k2, i: _dot(jnp.concatenate([k2, key_off], axis=1), qm_ref[i]),
             lambda s2: s2 * off)
        return carry

    lax.fori_loop(0, qi * n_diag, full_tile, 0)
    for d in range(n_diag):
        qs = slice(d * tk, tq)
        r = lax.broadcasted_iota(jnp.int32, (tk, tq - d * tk), 0)
        c = lax.broadcasted_iota(jnp.int32, (tk, tq - d * tk), 1)
        dist = jnp.abs(r - c).astype(F32)
        visible = (lax.shift_right_logical(r, CHUNK_SHIFT)
                   <= lax.shift_right_logical(c, CHUNK_SHIFT))

        def diag_scores(k2, i):
            slope2 = _alibi_slope(i // 2, n_heads) * LOG2E
            return jnp.where(visible, _dot(k2, qm_ref[i, :LANES, qs]) - slope2 * dist, NEG_INF)

        tile(pl.multiple_of(qi * tq + d * tk, tk), diag_scores, lambda s2: 0.0, qs)
    lam = _lambda_scalar(lq1_ref[...], lk1_ref[...], lq2_ref[...], lk2_ref[...], lambda_init)
    for h in range(n_heads):
        blk = slice(h * LANES, (h + 1) * LANES)
        o = (_flash_result(2 * h, LANES, acc_ref) - lam * _flash_result(2 * h + 1, LANES, acc_ref)).T
        o_ref[:, blk] = _head_rms(o, g_ref[:, blk], lambda_init).astype(BF16)


def _diff_prompt(qt, k, vt, lams, g, *, batch, seq, tq, tk, n_heads, lambda_init):
    width = n_heads * 2 * DH_B
    nq = seq // tq
    return pl.pallas_call(
        functools.partial(_diff_prompt_kernel, tq=tq, tk=tk, n_heads=n_heads,
                          lambda_init=lambda_init),
        grid=(batch, nq),
        in_specs=[pl.BlockSpec((1, width, tq), lambda b, i: (b, 0, i)),
                  pl.BlockSpec((seq, width), lambda b, i: (b, 0)),
                  pl.BlockSpec((1, width, seq), lambda b, i: (b, 0, 0))]
                 + [_const_spec(a.shape) for a in lams] + [_const_spec(g.shape)],
        out_specs=pl.BlockSpec((tq, width), lambda b, i: (b * nq + i, 0)),
        out_shape=jax.ShapeDtypeStruct((batch * seq, width), BF16),
        scratch_shapes=_stream_scratch(2 * n_heads, LANES, tq),
        compiler_params=_cparams(2),
        name="diff_prompt",
    )(qt, k, vt, *lams, g)


def _lane_half_masks(rows):
    lane = lax.broadcasted_iota(jnp.int32, (rows, LANES), 1)
    lo = jnp.where(lane < HD_A, 1.0, 0.0)
    return lo, 1.0 - lo


def _masked_q(q2, mask):
    return (q2.astype(F32) * mask).astype(BF16)


def _flash_init(m_ref, l_ref, acc_ref):
    m_ref[...] = jnp.full(m_ref.shape, NEG_INF, F32)
    l_ref[...] = jnp.zeros(l_ref.shape, F32)
    acc_ref[...] = jnp.zeros(acc_ref.shape, F32)


def _flash_step(s, pv, m_ref, l_ref, acc_ref):
    m_prev = m_ref[...]
    m_new = jnp.maximum(m_prev, jnp.max(s, axis=1, keepdims=True))
    alpha = jnp.exp(m_prev - m_new)
    p = jnp.exp(s - m_new)
    l_ref[...] = alpha * l_ref[...] + jnp.sum(p, axis=1, keepdims=True)
    acc_ref[...] = alpha * acc_ref[...] + pv(p.astype(BF16))
    m_ref[...] = m_new


def _fox_decode_kernel(q_ref, kc_ref, vc_ref, kn_ref, vn_ref, ccol_ref, crow_ref, o_ref,
                       m_ref, l_ref, acc_ref, *, tq, tk, n_kv, n_heads):
    j = pl.program_id(1)
    lo, hi = _lane_half_masks(tq)

    @pl.when(j == 0)
    def _():
        _flash_init(m_ref, l_ref, acc_ref)

    def head_loop(scores_of, pv_of, bias_of, mask):
        pending = []
        for pair in range(n_heads // 2):
            blk = slice(pair * LANES, (pair + 1) * LANES)
            q2 = q_ref[:, blk]
            scores = scores_of(blk)
            pv = pv_of(blk)
            for half, hm in enumerate((lo, hi)):
                s = scores(_masked_q(q2, hm)) + bias_of(2 * pair + half)
                if mask is not None:
                    s = jnp.where(mask, s, NEG_INF)
                pending.append((s, pv))
        for h, (s, pv) in enumerate(pending):
            _flash_step(s, pv, m_ref.at[h], l_ref.at[h], acc_ref.at[h])

    @pl.when(j < n_kv)
    def _():
        ks = pl.multiple_of(j * tk, tk)

        def scores_of(blk):
            kt = kc_ref[0, 0, blk, :].astype(BF16)
            return lambda qm: _dot(qm, kt)

        def pv_of(blk):
            vt = vc_ref[0, 0, blk, :].astype(BF16)
            return lambda p: _dot_nt(p, vt)

        head_loop(scores_of, pv_of,
                  lambda h: ccol_ref[0, :, h:h + 1] - crow_ref[0, h:h + 1, pl.ds(ks, tk)], None)

    @pl.when(j == n_kv)
    def _():
        past = n_kv * tk
        q_pos = lax.broadcasted_iota(jnp.int32, (tq, tq), 0)
        k_pos = lax.broadcasted_iota(jnp.int32, (tq, tq), 1)
        head_loop(lambda blk: (lambda qm: _dot_nt(qm, kn_ref[:, blk])),
                  lambda blk: (lambda p: _dot(p, vn_ref[:, blk])),
                  lambda h: ccol_ref[0, :, h:h + 1] - crow_ref[0, h:h + 1, past:past + tq],
                  k_pos <= q_pos)
        for pair in range(n_heads // 2):
            blk = slice(pair * LANES, (pair + 1) * LANES)
            o0 = acc_ref[2 * pair] / l_ref[2 * pair]
            o1 = acc_ref[2 * pair + 1] / l_ref[2 * pair + 1]
            o_ref[:, blk] = (o0 * lo + o1 * hi).astype(BF16)


def _fox_decode(q, kt_cache, vt_cache, k_new, v_new, c_col, c_row, *, layer, tq, tk, n_heads):
    _, batch, width, past = kt_cache.shape
    n_kv = past // tk
    new_blk = pl.BlockSpec((tq, width), lambda b, j: (b, 0))
    cache_blk = pl.BlockSpec((1, 1, width, tk),
                             lambda b, j: (layer, b, 0, jnp.minimum(j, n_kv - 1)))
    return pl.pallas_call(
        functools.partial(_fox_decode_kernel, tq=tq, tk=tk, n_kv=n_kv, n_heads=n_heads),
        grid=(batch, n_kv + 1),
        in_specs=[new_blk, cache_blk, cache_blk, new_blk, new_blk,
                  pl.BlockSpec((1, tq, n_heads), lambda b, j: (b, past // tq, 0)),
                  pl.BlockSpec((1, n_heads, c_row.shape[2]), lambda b, j: (b, 0, 0))],
        out_specs=new_blk,
        out_shape=jax.ShapeDtypeStruct((batch * tq, width), BF16),
        scratch_shapes=[pltpu.VMEM((n_heads, tq, 1), F32), pltpu.VMEM((n_heads, tq, 1), F32),
                        pltpu.VMEM((n_heads, tq, LANES), F32)],
        compiler_params=_cparams(2),
        name="fox_decode",
    )(q, kt_cache, vt_cache, k_new, v_new, c_col, c_row)


def _diff_decode_kernel(q_ref, kc_ref, vc_ref, kn_ref, vn_ref, lq1_ref, lk1_ref, lq2_ref, lk2_ref,
                        g_ref, o_ref, m_ref, l_ref, acc_ref,
                        *, tq, tk, n_kv, n_heads, lambda_init):
    j = pl.program_id(1)
    lo, hi = _lane_half_masks(tq)
    past = n_kv * tk
    q_abs = (past + lax.broadcasted_iota(jnp.int32, (tq, 1), 0)).astype(F32)

    @pl.when(j == 0)
    def _():
        _flash_init(m_ref, l_ref, acc_ref)

    def head_loop(scores_of, pv_of, dist):
        pending = []
        for h in range(n_heads):
            blk = slice(h * LANES, (h + 1) * LANES)
            slope = _alibi_slope(h, n_heads)
            q2 = q_ref[:, blk]
            scores = scores_of(blk)
            pv = pv_of(h)
            for hm in (lo, hi):
                pending.append((scores(_masked_q(q2, hm)) - slope * dist, pv))
        for idx, (s, pv) in enumerate(pending):
            _flash_step(s, pv, m_ref.at[idx], l_ref.at[idx], acc_ref.at[idx])

    @pl.when(j < n_kv)
    def _():
        k_abs = (j * tk + lax.broadcasted_iota(jnp.int32, (1, tk), 1)).astype(F32)

        def scores_of(blk):
            kt = kc_ref[0, 0, blk, :].astype(BF16)
            return lambda qm: _dot(qm, kt)

        def pv_of(h):
            v2 = vc_ref[0, 0, pl.ds(h, tk, stride=n_heads), :].astype(BF16)
            return lambda p: _dot(p, v2)

        head_loop(scores_of, pv_of, jnp.abs(q_abs - k_abs))

    @pl.when(j == n_kv)
    def _():
        assert past % CHUNK == 0 and tq <= CHUNK
        k_abs = (past + lax.broadcasted_iota(jnp.int32, (1, tq), 1)).astype(F32)
        head_loop(lambda blk: (lambda qm: _dot_nt(qm, kn_ref[:, blk])),
                  lambda h: (lambda p: _dot(p, vn_ref[:, h * LANES:(h + 1) * LANES])),
                  jnp.abs(q_abs - k_abs))
        lam = _lambda_scalar(lq1_ref[...], lk1_ref[...], lq2_ref[...], lk2_ref[...], lambda_init)
        for h in range(n_heads):
            blk = slice(h * LANES, (h + 1) * LANES)
            o = acc_ref[2 * h] / l_ref[2 * h] - lam * (acc_ref[2 * h + 1] / l_ref[2 * h + 1])
            o_ref[:, blk] = _head_rms(o, g_ref[:, blk], lambda_init).astype(BF16)


def _diff_decode(q, kt_cache, v_cache, k_new, v_new, lams, g, *, layer, tq, tk, n_heads,
                 lambda_init):
    _, batch, width, past = kt_cache.shape
    n_kv = past // tk
    new_blk = pl.BlockSpec((tq, width), lambda b, j: (b, 0))
    return pl.pallas_call(
        functools.partial(_diff_decode_kernel, tq=tq, tk=tk, n_kv=n_kv, n_heads=n_heads,
                          lambda_init=lambda_init),
        grid=(batch, n_kv + 1),
        in_specs=[new_blk,
                  pl.BlockSpec((1, 1, width, tk),
                               lambda b, j: (layer, b, 0, jnp.minimum(j, n_kv - 1))),
                  pl.BlockSpec((1, 1, tk * n_heads, LANES),
                               lambda b, j: (layer, b, jnp.minimum(j, n_kv - 1), 0)),
                  new_blk, new_blk]
                 + [_const_spec(a.shape) for a in lams] + [_const_spec(g.shape)],
        out_specs=new_blk,
        out_shape=jax.ShapeDtypeStruct((batch * tq, width), BF16),
        scratch_shapes=[pltpu.VMEM((2 * n_heads, tq, 1), F32),
                        pltpu.VMEM((2 * n_heads, tq, 1), F32),
                        pltpu.VMEM((2 * n_heads, tq, LANES), F32)],
        compiler_params=_cparams(2),
        name="diff_decode",
    )(q, kt_cache, v_cache, k_new, v_new, *lams, g)


def _post_kernel(x_ref, oa_ref, ob_ref, wgt_ref, wba_ref, wbb_ref, wo_ref, g_ref, b_ref, o_ref,
                 *, alpha):
    x = x_ref[...]
    xb = x.astype(BF16)
    d = x.shape[1]
    merged = _sigmoid(_dot_nt(xb, wgt_ref[:d, :])) * _dot(oa_ref[...], wba_ref[...])
    merged = merged + _sigmoid(_dot_nt(xb, wgt_ref[d:, :])) * _dot(ob_ref[...], wbb_ref[...])
    y = alpha * x + _dot(merged.astype(BF16), wo_ref[...])
    o_ref[...] = _layer_norm(y, g_ref[...], b_ref[...])


def _post(x, oa, ob, wt_gate, w_ba, w_bb, w_o, g, b, *, layer, tm, alpha):
    m, d = x.shape
    row = lambda i: (i, 0)
    return pl.pallas_call(
        functools.partial(_post_kernel, alpha=alpha),
        grid=(m // tm,),
        in_specs=[pl.BlockSpec((tm, d), row), pl.BlockSpec((tm, oa.shape[1]), row),
                  pl.BlockSpec((tm, ob.shape[1]), row), _const_spec(wt_gate.shape)]
                 + [_layer_spec(a.shape, layer) for a in (w_ba, w_bb, w_o, g, b)],
        out_specs=pl.BlockSpec((tm, d), row),
        out_shape=jax.ShapeDtypeStruct((m, d), F32),
        compiler_params=_cparams(1),
        name="merge_outproj_ln",
    )(x, oa, ob, wt_gate, w_ba, w_bb, w_o, g, b)


def _ffn_kernel(x_ref, p_ref, wg_ref, wu_ref, wd_ref, wpg_ref, wpp_ref, g_ref, b_ref, o_ref,
                *, alpha):
    x = x_ref[...]
    xb = x.astype(BF16)
    hg = _dot(xb, wg_ref[...])
    hidden = (hg * _sigmoid(hg) * _dot(xb, wu_ref[...])).astype(BF16)
    y = alpha * x + _dot(hidden, wd_ref[...])
    ple = _sigmoid(_dot(xb, wpg_ref[...])) * _dot(p_ref[...].astype(BF16), wpp_ref[...])
    o_ref[...] = _layer_norm(y + ple, g_ref[...], b_ref[...])


def _ffn(x, p, w_g, w_u, w_d, w_pg, w_pp, g, b, *, layer, tm, alpha):
    m, d = x.shape
    row = lambda i: (i, 0)
    return pl.pallas_call(
        functools.partial(_ffn_kernel, alpha=alpha),
        grid=(m // tm,),
        in_specs=[pl.BlockSpec((tm, d), row),
                  pl.BlockSpec((None, tm, p.shape[2]), lambda i: (layer, i, 0))]
                 + [_layer_spec(a.shape, layer) for a in (w_g, w_u, w_d, w_pg, w_pp, g, b)],
        out_specs=pl.BlockSpec((tm, d), row),
        out_shape=jax.ShapeDtypeStruct((m, d), F32),
        compiler_params=_cparams(1),
        name="swiglu_ple_ln",
    )(x, p, w_g, w_u, w_d, w_pg, w_pp, g, b)


def _row_tile(m):
    for tm in (512, 256, 128):
        if m % tm == 0:
            return tm
    raise ValueError(f"row count {m} is not a multiple of 128")


def kernel(x_prompt, x_sample, p_prompt, p_sample, cache_fox_k, cache_fox_v, cache_fox_logf, cache_diff_k, cache_diff_v, w_in, b_forget, lambda_q1, lambda_k1, lambda_q2, lambda_k2, diff_norm_g, w_branch_fox, w_branch_diff, w_out, ln1_g, ln1_b, w_ffn_gate, w_ffn_up, w_ffn_down, w_ple_gate, w_ple_proj, ln2_g, ln2_b):
    batch, seq, d_model = x_prompt.shape
    dec_batch, dec_seq, _ = x_sample.shape
    depth = w_in.shape[0]
    past = cache_fox_k.shape[2]
    h_a = cache_fox_k.shape[3]
    h_b = cache_diff_k.shape[3]
    w_a = h_a * HD_A
    w_b = h_b * 2 * DH_B
    assert w_a == w_b and w_in.shape[2] == 3 * w_a + h_a + 3 * w_b + 2 * d_model
    assert HD_A == DH_B
    alpha = (2 * depth) ** 0.25
    m_p = batch * seq
    m_s = dec_batch * dec_seq
    tm_p = _row_tile(seq)
    tm_s = _row_tile(m_s)
    tq, tk = min(512, seq), min(256, seq)
    tk_dec = min(2048, past)
    assert seq % tq == 0 and tq % tk == 0 and tk % CHUNK == 0 and past % tk_dec == 0
    t_dec = past + dec_seq
    t_dec_pad = -(-t_dec // SCAN_BLOCK) * SCAN_BLOCK
    qk_scale = HD_A ** -0.5
    f_pad = 16

    xp = x_prompt.reshape(m_p, d_model)
    xs = x_sample.reshape(m_s, d_model)

    wt_in = jnp.swapaxes(w_in, 1, 2)
    kt_cache_a = jnp.transpose(cache_fox_k, (0, 1, 3, 4, 2)).reshape(depth, dec_batch, w_a, past)
    vt_cache_a = jnp.transpose(cache_fox_v, (0, 1, 3, 4, 2)).reshape(depth, dec_batch, w_a, past)
    kt_cache_b = jnp.transpose(cache_diff_k, (0, 1, 3, 4, 5, 2)).reshape(depth, dec_batch, w_b, past)
    lf_cache = jnp.swapaxes(cache_fox_logf, 2, 3).astype(F32)
    v_cache_b = cache_diff_v.reshape(depth, dec_batch, past * h_b, 2 * DH_B)

    post_w = (w_branch_fox.astype(BF16), w_branch_diff.astype(BF16), w_out.astype(BF16),
              ln1_g.reshape(depth, 1, d_model), ln1_b.reshape(depth, 1, d_model))
    ffn_w = (w_ffn_gate.astype(BF16), w_ffn_up.astype(BF16), w_ffn_down.astype(BF16),
             w_ple_gate.astype(BF16), w_ple_proj.astype(BF16),
             ln2_g.reshape(depth, 1, d_model), ln2_b.reshape(depth, 1, d_model))
    pp = p_prompt.reshape(depth, m_p, -1)
    ps = p_sample.reshape(depth, m_s, -1)

    o_f = 3 * w_a
    o_qb = o_f + h_a
    o_ga = o_qb + 3 * w_b

    stacked = None
    new_s = [[] for _ in range(5)]
    for l in range(depth):
        lambda_init = 0.8 - 0.6 * math.exp(-0.3 * l)
        wt = wt_in[l]
        wt_qkv = jnp.concatenate([wt[:o_f], wt[o_qb:o_ga]], axis=0).astype(BF16)
        wt_f = jnp.pad(wt[o_f:o_qb], ((0, f_pad - h_a), (0, 0))).astype(BF16)
        b_f = jnp.pad(b_forget[l], (0, f_pad - h_a))
        wt_gate = wt[o_ga:].astype(BF16)

        (qat, ka, kat, vat, vat_b, qbt, kb, kbt, vb, vbt_b, logf_p) = _inproj_prompt(
            xp, wt_qkv, wt_f, b_f.reshape(f_pad, 1), stacked, layer=l, depth=depth, batch=batch,
            seq=seq, tm=tm_p, n_heads=h_a, q_scale=qk_scale * LOG2E)
        stacked = (kat, vat, kbt, vb, logf_p)
        (qa_s, qb_s, ka_s, va_s, kb_s, vb_s, ka_sb, va_sb, kb_sb, vb_sb, logf_s) = _inproj_decode(
            xs, wt_qkv, wt_f, b_f.reshape(1, f_pad), n_heads=h_a, q_scale=qk_scale)

        c_row_p, aug_p = _cumsum_aug(logf_p[l], LOG2E, width=w_a)
        lf_s = jnp.swapaxes(logf_s.reshape(dec_batch, dec_seq, h_a), 1, 2)
        lf_all = jnp.concatenate([lf_cache[l], lf_s], axis=2)
        lf_all = jnp.pad(lf_all, ((0, 0), (0, 0), (0, t_dec_pad - t_dec)))
        c_row_s = _cumsum_rows(lf_all.reshape(dec_batch * h_a, t_dec_pad), 1.0)
        c_row_s = c_row_s.reshape(dec_batch, h_a, t_dec_pad)
        c_col_s = jnp.swapaxes(c_row_s, 1, 2)

        lams = [a[l].reshape(1, DH_B) for a in (lambda_q1, lambda_k1, lambda_q2, lambda_k2)]
        g_diff = diff_norm_g[l].reshape(1, w_b)

        oa_p = _fox_prompt(qat, ka, vat_b, c_row_p, aug_p,
                           batch=batch, seq=seq, tq=tq, tk=tk, n_heads=h_a)
        ob_p = _diff_prompt(qbt, kb, vbt_b, lams, g_diff, batch=batch, seq=seq, tq=tq, tk=tk,
                            n_heads=h_b, lambda_init=lambda_init)
        oa_s = _fox_decode(qa_s, kt_cache_a, vt_cache_a, ka_sb, va_sb, c_col_s, c_row_s,
                           layer=l, tq=dec_seq, tk=tk_dec, n_heads=h_a)
        ob_s = _diff_decode(qb_s, kt_cache_b, v_cache_b, kb_sb, vb_sb, lams, g_diff,
                            layer=l, tq=dec_seq, tk=tk_dec, n_heads=h_b, lambda_init=lambda_init)

        xp = _post(xp, oa_p, ob_p, wt_gate, *post_w, layer=l, tm=tm_p, alpha=alpha)
        xp = _ffn(xp, pp, *ffn_w, layer=l, tm=tm_p, alpha=alpha)
        xs = _post(xs, oa_s, ob_s, wt_gate, *post_w, layer=l, tm=tm_s, alpha=alpha)
        xs = _ffn(xs, ps, *ffn_w, layer=l, tm=tm_s, alpha=alpha)

        for lst, r in zip(new_s, (ka_s, va_s, logf_s, kb_s, vb_s)):
            lst.append(r)

    kat, vat, kbt, vb, logf_p = stacked
    ka_s, va_s, logf_s, kb_s, vb_s = [jnp.stack(a) for a in new_s]
    fox_k_p = jnp.transpose(kat.reshape(depth, batch, h_a, HD_A, seq), (0, 1, 4, 2, 3))
    fox_v_p = jnp.transpose(vat.reshape(depth, batch, h_a, HD_A, seq), (0, 1, 4, 2, 3))
    fox_lf_p = jnp.swapaxes(logf_p, 2, 3)
    diff_k_p = jnp.transpose(kbt.reshape(depth, batch, h_b, 2, DH_B, seq), (0, 1, 5, 2, 3, 4))
    return (xp.reshape(batch, seq, d_model), xs.reshape(dec_batch, dec_seq, d_model),
            fox_k_p, fox_v_p, fox_lf_p, diff_k_p, vb,
            ka_s.reshape(depth, dec_batch, dec_seq, h_a, HD_A),
            va_s.reshape(depth, dec_batch, dec_seq, h_a, HD_A),
            logf_s.reshape(depth, dec_batch, dec_seq, h_a),
            kb_s.reshape(depth, dec_batch, dec_seq, h_b, 2, DH_B),
            vb_s.reshape(depth, dec_batch, dec_seq, h_b, 2 * DH_B))
```

```python
import functools
import math

import jax
import jax.numpy as jnp
import numpy as np
from jax import lax
from jax.experimental import pallas as pl
from jax.experimental.pallas import tpu as pltpu

F32 = jnp.float32
BF16 = jnp.bfloat16

HD_A = 64
DH_B = 64
CHUNK = 64
CHUNK_SHIFT = 6
LN_EPS = 1e-5
RMS_EPS = 1e-5
NEG_INF = -1e30
LOG2E = math.log2(math.e)
LANES = 128
VMEM_LIMIT = 56 * 1024 * 1024


def _cparams(n_axes):
    return pltpu.CompilerParams(dimension_semantics=("arbitrary",) * n_axes,
                                vmem_limit_bytes=VMEM_LIMIT)


def _const_spec(shape):
    zeros = (0,) * len(shape)
    return pl.BlockSpec(shape, lambda *_: zeros, pipeline_mode=pl.Buffered(1))


def _layer_spec(shape, layer):
    zeros = (0,) * (len(shape) - 1)
    return pl.BlockSpec((None,) + tuple(shape[1:]), lambda *_: (layer,) + zeros,
                        pipeline_mode=pl.Buffered(1))


def _dot(a, b):
    return jnp.dot(a, b, preferred_element_type=F32)


def _dot_nt(a, b):
    return lax.dot_general(a, b, (((1,), (1,)), ((), ())), preferred_element_type=F32)


def _sigmoid(x):
    return 1.0 / (1.0 + jnp.exp(-x))


def _log_sigmoid(z):
    return jnp.minimum(z, 0.0) - jnp.log1p(jnp.exp(-jnp.abs(z)))


def _layer_norm(y, g, b):
    mu = jnp.mean(y, axis=-1, keepdims=True)
    yc = y - mu
    var = jnp.mean(yc * yc, axis=-1, keepdims=True)
    return yc * lax.rsqrt(var + LN_EPS) * g + b


def _inproj_prompt_kernel(x_ref, wt_ref, wft_ref, bf_ref, *refs, width, n_heads, q_scale):
    (qat_ref, ka_ref, kat_ref, vat_ref, vatb_ref,
     qbt_ref, kb_ref, kbt_ref, vb_ref, vbtb_ref, logf_ref) = refs[-11:]
    xb = x_ref[...].astype(BF16)

    def proj_t(i):
        return _dot_nt(wt_ref[i * width:(i + 1) * width, :], xb)

    qat_ref[0] = (proj_t(0) * q_scale).astype(BF16)
    kat = proj_t(1)
    kat_ref[0, 0] = kat
    ka_ref[...] = kat.T.astype(BF16)
    vat = proj_t(2)
    vat_ref[0, 0] = vat
    vatb_ref[0] = vat.astype(BF16)
    qbt_ref[0] = (proj_t(3) * q_scale).astype(BF16)
    kbt = proj_t(4)
    kbt_ref[0, 0] = kbt
    kb_ref[...] = kbt.T.astype(BF16)
    vbt = proj_t(5)
    vbtb_ref[0] = vbt.astype(BF16)
    vb = vbt.T
    for h in range(width // LANES):
        vb_ref[0, 0, :, h, :] = vb[:, h * LANES:(h + 1) * LANES]
    z = _dot_nt(wft_ref[...], xb) + bf_ref[...]
    logf_ref[0, 0] = _log_sigmoid(z)[:n_heads, :]


def _inproj_prompt(x, wt_qkv, wt_f, b_f_col, stacked, *, layer, depth, batch, seq, tm, n_heads,
                   q_scale):
    m, d = x.shape
    width = wt_qkv.shape[0] // 6
    nt = seq // tm
    row = lambda i: (i, 0)
    tr = lambda i: (i // nt, 0, i % nt)
    tr_l = lambda i: (layer, i // nt, 0, i % nt)
    t_blk = pl.BlockSpec((1, width, tm), tr)
    t_blk_l = pl.BlockSpec((1, 1, width, tm), tr_l)
    n_blk = pl.BlockSpec((tm, width), row)
    t_f32 = jax.ShapeDtypeStruct((depth, batch, width, seq), F32)
    t_b16 = jax.ShapeDtypeStruct((batch, width, seq), BF16)
    n_b16 = jax.ShapeDtypeStruct((m, width), BF16)
    vb_heads = width // LANES
    out_specs = [t_blk, n_blk, t_blk_l, t_blk_l, t_blk, t_blk, n_blk, t_blk_l,
                 pl.BlockSpec((1, 1, tm, vb_heads, LANES),
                              lambda i: (layer, i // nt, i % nt, 0, 0)),
                 t_blk, pl.BlockSpec((1, 1, n_heads, tm), tr_l)]
    out_shape = [t_b16, n_b16, t_f32, t_f32, t_b16, t_b16, n_b16, t_f32,
                 jax.ShapeDtypeStruct((depth, batch, seq, vb_heads, LANES), F32),
                 t_b16, jax.ShapeDtypeStruct((depth, batch, n_heads, seq), F32)]
    stacked_outputs = (2, 3, 7, 8, 10)
    in_specs = [pl.BlockSpec((tm, d), row), _const_spec(wt_qkv.shape), _const_spec(wt_f.shape),
                _const_spec(b_f_col.shape)]
    args = [x, wt_qkv, wt_f, b_f_col]
    aliases = {len(args) + i: o for i, o in enumerate(stacked_outputs)}
    in_specs += [pl.BlockSpec(memory_space=pl.ANY)] * len(stacked)
    args += list(stacked)
    return pl.pallas_call(
        functools.partial(_inproj_prompt_kernel, width=width, n_heads=n_heads, q_scale=q_scale),
        grid=(m // tm,),
        in_specs=in_specs,
        out_specs=out_specs,
        out_shape=out_shape,
        input_output_aliases=aliases,
        compiler_params=_cparams(1),
        name="inproj_prompt",
    )(*args)


def _inproj_decode_kernel(x_ref, wt_ref, wft_ref, bf_ref,
                          qa_ref, qb_ref, ka_ref, va_ref, kb_ref, vb_ref,
                          kab_ref, vab_ref, kbb_ref, vbb_ref, logf_ref, *, width, n_heads, q_scale):
    xb = x_ref[...].astype(BF16)

    def proj(i):
        return _dot_nt(xb, wt_ref[i * width:(i + 1) * width, :])

    qa_ref[...] = (proj(0) * q_scale).astype(BF16)
    qb_ref[...] = (proj(3) * q_scale).astype(BF16)
    for i, (full_ref, half_ref) in ((1, (ka_ref, kab_ref)), (2, (va_ref, vab_ref)),
                                    (4, (kb_ref, kbb_ref)), (5, (vb_ref, vbb_ref))):
        val = proj(i)
        full_ref[...] = val
        half_ref[...] = val.astype(BF16)
    z = _dot_nt(xb, wft_ref[...]) + bf_ref[...]
    logf_ref[...] = _log_sigmoid(z)[:, :n_heads]


def _inproj_decode(x, wt_qkv, wt_f, b_f_row, *, n_heads, q_scale):
    m, d = x.shape
    width = wt_qkv.shape[0] // 6
    out_shape = ([jax.ShapeDtypeStruct((m, width), BF16)] * 2
                 + [jax.ShapeDtypeStruct((m, width), F32)] * 4
                 + [jax.ShapeDtypeStruct((m, width), BF16)] * 4
                 + [jax.ShapeDtypeStruct((m, n_heads), F32)])
    return pl.pallas_call(
        functools.partial(_inproj_decode_kernel, width=width, n_heads=n_heads, q_scale=q_scale),
        out_shape=out_shape,
        compiler_params=pltpu.CompilerParams(vmem_limit_bytes=VMEM_LIMIT),
        name="inproj_decode",
    )(x, wt_qkv, wt_f, b_f_row)


SCAN_BLOCK = 256


def _split3(a):
    a1 = a.astype(BF16)
    r1 = a - a1.astype(F32)
    a2 = r1.astype(BF16)
    a3 = (r1 - a2.astype(F32)).astype(BF16)
    return a1, a2, a3


def _split3_const(x):
    pieces, rest = [], np.float32(x)
    for _ in range(3):
        p = np.float32(np.asarray(rest, dtype=jnp.bfloat16))
        pieces.append(float(p))
        rest = np.float32(rest - p)
    return pieces


def _cumsum_kernel(lf_ref, c_ref, *, n_blk, scale):
    rows = lf_ref.shape[0]
    r = lax.broadcasted_iota(jnp.int32, (SCAN_BLOCK, SCAN_BLOCK), 0)
    c = lax.broadcasted_iota(jnp.int32, (SCAN_BLOCK, SCAN_BLOCK), 1)
    upper = jnp.where(r <= c, 1.0, 0.0).astype(BF16)
    carry = jnp.zeros((rows, 1), F32)
    for j in range(n_blk):
        sl = slice(j * SCAN_BLOCK, (j + 1) * SCAN_BLOCK)
        a1, a2, a3 = _split3(lf_ref[:, sl])
        blk = _dot(a1, upper) + _dot(a2, upper) + _dot(a3, upper) + carry
        c_ref[:, sl] = blk * scale
        carry = blk[:, SCAN_BLOCK - 1:SCAN_BLOCK]


def _cumsum_rows(lf_rows, scale):
    rows, t = lf_rows.shape
    assert t % SCAN_BLOCK == 0
    return pl.pallas_call(
        functools.partial(_cumsum_kernel, n_blk=t // SCAN_BLOCK, scale=scale),
        out_shape=jax.ShapeDtypeStruct((rows, t), F32),
        compiler_params=pltpu.CompilerParams(vmem_limit_bytes=VMEM_LIMIT),
        name="logf_cumsum",
    )(lf_rows)


AUG_PIECES = 3


def _aug_lane(h, piece):
    return AUG_PIECES * (h % 2) + piece


def _cumsum_aug_kernel(lf_rows_ref, lf_cols_ref, place_ref, c_ref, aug_ref, *, n_blk, scale):
    rows = lf_rows_ref.shape[0]
    batch, _, width = aug_ref.shape
    r = lax.broadcasted_iota(jnp.int32, (SCAN_BLOCK, SCAN_BLOCK), 0)
    c = lax.broadcasted_iota(jnp.int32, (SCAN_BLOCK, SCAN_BLOCK), 1)
    upper = jnp.where(r <= c, 1.0, 0.0).astype(BF16)
    lower = jnp.where(c <= r, 1.0, 0.0).astype(BF16)
    carry_r = jnp.zeros((rows, 1), F32)
    carry_c = jnp.zeros((1, rows), F32)
    for j in range(n_blk):
        sl = slice(j * SCAN_BLOCK, (j + 1) * SCAN_BLOCK)
        a1, a2, a3 = _split3(lf_rows_ref[:, sl])
        blk_r = _dot(a1, upper) + _dot(a2, upper) + _dot(a3, upper) + carry_r
        c_ref[:, sl] = blk_r * scale
        carry_r = blk_r[:, SCAN_BLOCK - 1:SCAN_BLOCK]
        b1, b2, b3 = _split3(lf_cols_ref[sl, :])
        blk_c = _dot(lower, b1) + _dot(lower, b2) + _dot(lower, b3) + carry_c
        carry_c = blk_c[SCAN_BLOCK - 1:SCAN_BLOCK, :]
        pieces = jnp.concatenate(_split3(blk_c * scale), axis=1)
        placed = _dot(pieces, place_ref[...]).astype(BF16)
        for b in range(batch):
            aug_ref[b, sl, :] = placed[:, b * width:(b + 1) * width]


def _placement_matrix(batch, n_heads, width):
    rows = batch * n_heads
    place = np.zeros((AUG_PIECES * rows, batch * width), np.float32)
    for b in range(batch):
        for h in range(n_heads):
            for p in range(AUG_PIECES):
                place[p * rows + b * n_heads + h,
                      b * width + (h // 2) * LANES + _aug_lane(h, p)] = 1.0
    return jnp.asarray(place, BF16)


def _cumsum_aug(lf, scale, *, width):
    batch, n_heads, t = lf.shape
    assert t % SCAN_BLOCK == 0
    rows = batch * n_heads
    c_rows, aug = pl.pallas_call(
        functools.partial(_cumsum_aug_kernel, n_blk=t // SCAN_BLOCK, scale=scale),
        out_shape=[jax.ShapeDtypeStruct((rows, t), F32),
                   jax.ShapeDtypeStruct((batch, t, width), BF16)],
        compiler_params=pltpu.CompilerParams(vmem_limit_bytes=VMEM_LIMIT),
        name="logf_cumsum_aug",
    )(lf.reshape(rows, t), jnp.transpose(lf, (2, 0, 1)).reshape(t, rows),
      _placement_matrix(batch, n_heads, width))
    return c_rows.reshape(batch, n_heads, t), aug


def _lambda_scalar(lq1, lk1, lq2, lk2, lambda_init):
    return (jnp.exp(jnp.sum(lq1 * lk1, axis=1, keepdims=True))
            - jnp.exp(jnp.sum(lq2 * lk2, axis=1, keepdims=True)) + lambda_init)


def _head_rms(o, g_row, lambda_init):
    o = o * lax.rsqrt(jnp.mean(o * o, axis=-1, keepdims=True) + RMS_EPS)
    return o * g_row * (1.0 - lambda_init)


def _alibi_slope(h, n_heads):
    return 2.0 ** (-8.0 * (h + 1) / n_heads)


def _sublane_half_masks(cols):
    row = lax.broadcasted_iota(jnp.int32, (LANES, cols), 0)
    lo = jnp.where(row < HD_A, 1.0, 0.0)
    return lo, 1.0 - lo


SUM_ROWS = 16


def _with_sum_rows(v_t):
    return jnp.concatenate([v_t, jnp.ones((SUM_ROWS, v_t.shape[1]), BF16)], axis=0)


def _flash_step_refs(i, t, shift, v_aug, m_ref, acc_ref, qs=slice(None)):
    m = m_ref[i, :, qs]
    m_new = jnp.maximum(m, jnp.max(t, axis=0, keepdims=True) + shift)
    p = jnp.exp2(t - (m_new - shift)).astype(BF16)
    acc_ref[i, :, qs] = jnp.exp2(m - m_new) * acc_ref[i, :, qs] + _dot(v_aug, p)
    m_ref[i, :, qs] = m_new


def _flash_init_refs(m_ref, acc_ref):
    m_ref[...] = jnp.full(m_ref.shape, NEG_INF, F32)
    acc_ref[...] = jnp.zeros(acc_ref.shape, F32)


def _flash_result(i, d, acc_ref):
    acc = acc_ref[i]
    return acc[:d] / acc[d:d + 1]


def _stream_scratch(n_streams, d, tq):
    return [pltpu.VMEM((n_streams, 2 * LANES, tq), BF16), pltpu.VMEM((n_streams, 1, tq), F32),
            pltpu.VMEM((n_streams, d + SUM_ROWS, tq), F32)]


def _fox_prompt_kernel(qt_ref, k_ref, vt_ref, crow_ref, aug_ref, o_ref, qm_ref, m_ref, acc_ref,
                       *, tq, tk, n_heads):
    qi = pl.program_id(1)
    n_diag = tq // tk
    lo, hi = _sublane_half_masks(tq)
    row = lax.broadcasted_iota(jnp.int32, (LANES, tq), 0)
    for h in range(n_heads):
        blk = slice((h // 2) * LANES, (h // 2 + 1) * LANES)
        qm_ref[h, :LANES] = (qt_ref[0, blk, :].astype(F32) * (lo, hi)[h % 2]).astype(BF16)
        picks = (row >= _aug_lane(h, 0)) & (row < _aug_lane(h, AUG_PIECES))
        qm_ref[h, LANES:] = jnp.where(picks, -1.0, 0.0).astype(BF16)
    _flash_init_refs(m_ref, acc_ref)

    def tile(ks, mask, qs=slice(None)):
        v_all = vt_ref[0, :, pl.ds(ks, tk)]
        scores = []
        for h in range(n_heads):
            blk = slice((h // 2) * LANES, (h // 2 + 1) * LANES)
            k_aug = jnp.concatenate([k_ref[pl.ds(ks, tk), blk], aug_ref[0, pl.ds(ks, tk), blk]],
                                    axis=1)
            t = _dot(k_aug, qm_ref[h, :, qs])
            if mask is not None:
                t = jnp.where(mask, t, NEG_INF)
            scores.append(t)
        for h in range(n_heads):
            _flash_step_refs(h, scores[h], crow_ref[0, h:h + 1, qs],
                             _with_sum_rows(v_all[h * HD_A:(h + 1) * HD_A]), m_ref, acc_ref, qs)

    def full_tile(j, carry):
        tile(pl.multiple_of(j * tk, tk), None)
        return carry

    lax.fori_loop(0, qi * n_diag, full_tile, 0)
    for d in range(n_diag):
        nq = tq - d * tk
        r = lax.broadcasted_iota(jnp.int32, (tk, nq), 0)
        c = lax.broadcasted_iota(jnp.int32, (tk, nq), 1)
        tile(pl.multiple_of(qi * tq + d * tk, tk), r <= c, slice(d * tk, tq))
    for pair in range(n_heads // 2):
        blk = slice(pair * LANES, (pair + 1) * LANES)
        o_t = jnp.concatenate([_flash_result(2 * pair, HD_A, acc_ref),
                               _flash_result(2 * pair + 1, HD_A, acc_ref)], axis=0)
        o_ref[:, blk] = o_t.T.astype(BF16)


def _fox_prompt(qt, k, vt, c_row, aug, *, batch, seq, tq, tk, n_heads):
    width = n_heads * HD_A
    nq = seq // tq
    return pl.pallas_call(
        functools.partial(_fox_prompt_kernel, tq=tq, tk=tk, n_heads=n_heads),
        grid=(batch, nq),
        in_specs=[pl.BlockSpec((1, width, tq), lambda b, i: (b, 0, i)),
                  pl.BlockSpec((seq, width), lambda b, i: (b, 0)),
                  pl.BlockSpec((1, width, seq), lambda b, i: (b, 0, 0)),
                  pl.BlockSpec((1, n_heads, tq), lambda b, i: (b, 0, i)),
                  pl.BlockSpec((1, seq, width), lambda b, i: (b, 0, 0))],
        out_specs=pl.BlockSpec((tq, width), lambda b, i: (b * nq + i, 0)),
        out_shape=jax.ShapeDtypeStruct((batch * seq, width), BF16),
        scratch_shapes=_stream_scratch(n_heads, HD_A, tq),
        compiler_params=_cparams(2),
        name="fox_prompt",
    )(qt, k, vt, c_row, aug)


def _diff_prompt_kernel(qt_ref, k_ref, vt_ref, lq1_ref, lk1_ref, lq2_ref, lk2_ref, g_ref, o_ref,
                        qm_ref, m_ref, acc_ref, *, tq, tk, n_heads, lambda_init):
    qi = pl.program_id(1)
    n_diag = tq // tk
    lo, hi = _sublane_half_masks(tq)
    row = lax.broadcasted_iota(jnp.int32, (LANES, tq), 0)
    for i in range(2 * n_heads):
        blk = slice((i // 2) * LANES, (i // 2 + 1) * LANES)
        qm_ref[i, :LANES] = (qt_ref[0, blk, :].astype(F32) * (lo, hi)[i % 2]).astype(BF16)
        s1, s2, s3 = _split3_const(_alibi_slope(i // 2, n_heads) * LOG2E)
        slope_rows = jnp.where(row == 0, s1, jnp.where(row == 1, s2, jnp.where(row == 2, s3, 0.0)))
        qm_ref[i, LANES:] = slope_rows.astype(BF16)
    _flash_init_refs(m_ref, acc_ref)
    assert tk <= 256
    lane = lax.broadcasted_iota(jnp.int32, (tk, LANES), 1)
    key_off = lax.broadcasted_iota(jnp.int32, (tk, LANES), 0).astype(F32)
    key_off = jnp.where(lane < AUG_PIECES, key_off, 0.0).astype(BF16)
    q_off = lax.broadcasted_iota(jnp.int32, (1, tq), 1).astype(F32)

    def tile(ks, scores_of, shift_of, qs=slice(None)):
        k_all = k_ref[pl.ds(ks, tk), :]
        v_all = vt_ref[0, :, pl.ds(ks, tk)]
        scores = [scores_of(k_all[:, (i // 2) * LANES:(i // 2 + 1) * LANES], i)
                  for i in range(2 * n_heads)]
        for i in range(2 * n_heads):
            h = i // 2
            _flash_step_refs(i, scores[i], shift_of(_alibi_slope(h, n_heads) * LOG2E),
                             _with_sum_rows(v_all[h * LANES:(h + 1) * LANES]), m_ref, acc_ref, qs)

    def full_tile(j, carry):
        off = (j * tk - qi * tq).astype(F32) - q_off
        tile(pl.multiple_of(j * tk, tk),
             lambda k2, i: _dot(jnp.concatenate([k2, key_off], axis=1), qm_ref[i]),
             lambda s2: s2 * off)
        return carry

    lax.fori_loop(0, qi * n_diag, full_tile, 0)
    for d in range(n_diag):
        qs = slice(d * tk, tq)
        r = lax.broadcasted_iota(jnp.int32, (tk, tq - d * tk), 0)
        c = lax.broadcasted_iota(jnp.int32, (tk, tq - d * tk), 1)
        dist = jnp.abs(r - c).astype(F32)
        visible = (lax.shift_right_logical(r, CHUNK_SHIFT)
                   <= lax.shift_right_logical(c, CHUNK_SHIFT))

        def diag_scores(k2, i):
            slope2 = _alibi_slope(i // 2, n_heads) * LOG2E
            return jnp.where(visible, _dot(k2, qm_ref[i, :LANES, qs]) - slope2 * dist, NEG_INF)

        tile(pl.multiple_of(qi * tq + d * tk, tk), diag_scores, lambda s2: 0.0, qs)
    lam = _lambda_scalar(lq1_ref[...], lk1_ref[...], lq2_ref[...], lk2_ref[...], lambda_init)
    for h in range(n_heads):
        blk = slice(h * LANES, (h + 1) * LANES)
        o = (_flash_result(2 * h, LANES, acc_ref) - lam * _flash_result(2 * h + 1, LANES, acc_ref)).T
        o_ref[:, blk] = _head_rms(o, g_ref[:, blk], lambda_init).astype(BF16)


def _diff_prompt(qt, k, vt, lams, g, *, batch, seq, tq, tk, n_heads, lambda_init):
    width = n_heads * 2 * DH_B
    nq = seq // tq
    return pl.pallas_call(
        functools.partial(_diff_prompt_kernel, tq=tq, tk=tk, n_heads=n_heads,
                          lambda_init=lambda_init),
        grid=(batch, nq),
        in_specs=[pl.BlockSpec((1, width, tq), lambda b, i: (b, 0, i)),
                  pl.BlockSpec((seq, width), lambda b, i: (b, 0)),
                  pl.BlockSpec((1, width, seq), lambda b, i: (b, 0, 0))]
                 + [_const_spec(a.shape) for a in lams] + [_const_spec(g.shape)],
        out_specs=pl.BlockSpec((tq, width), lambda b, i: (b * nq + i, 0)),
        out_shape=jax.ShapeDtypeStruct((batch * seq, width), BF16),
        scratch_shapes=_stream_scratch(2 * n_heads, LANES, tq),
        compiler_params=_cparams(2),
        name="diff_prompt",
    )(qt, k, vt, *lams, g)


def _lane_half_masks(rows):
    lane = lax.broadcasted_iota(jnp.int32, (rows, LANES), 1)
    lo = jnp.where(lane < HD_A, 1.0, 0.0)
    return lo, 1.0 - lo


def _masked_q(q2, mask):
    return (q2.astype(F32) * mask).astype(BF16)


def _flash_init(m_ref, l_ref, acc_ref):
    m_ref[...] = jnp.full(m_ref.shape, NEG_INF, F32)
    l_ref[...] = jnp.zeros(l_ref.shape, F32)
    acc_ref[...] = jnp.zeros(acc_ref.shape, F32)


def _flash_step(s, pv, m_ref, l_ref, acc_ref):
    m_prev = m_ref[...]
    m_new = jnp.maximum(m_prev, jnp.max(s, axis=1, keepdims=True))
    alpha = jnp.exp(m_prev - m_new)
    p = jnp.exp(s - m_new)
    l_ref[...] = alpha * l_ref[...] + jnp.sum(p, axis=1, keepdims=True)
    acc_ref[...] = alpha * acc_ref[...] + pv(p.astype(BF16))
    m_ref[...] = m_new


def _fox_decode_kernel(q_ref, kc_ref, vc_ref, kn_ref, vn_ref, ccol_ref, crow_ref, o_ref,
                       m_ref, l_ref, acc_ref, *, tq, tk, n_kv, n_heads):
    j = pl.program_id(1)
    lo, hi = _lane_half_masks(tq)

    @pl.when(j == 0)
    def _():
        _flash_init(m_ref, l_ref, acc_ref)

    def head_loop(scores_of, pv_of, bias_of, mask):
        pending = []
        for pair in range(n_heads // 2):
            blk = slice(pair * LANES, (pair + 1) * LANES)
            q2 = q_ref[:, blk]
            scores = scores_of(blk)
            pv = pv_of(blk)
            for half, hm in enumerate((lo, hi)):
                s = scores(_masked_q(q2, hm)) + bias_of(2 * pair + half)
                if mask is not None:
                    s = jnp.where(mask, s, NEG_INF)
                pending.append((s, pv))
        for h, (s, pv) in enumerate(pending):
            _flash_step(s, pv, m_ref.at[h], l_ref.at[h], acc_ref.at[h])

    def cache_tile():
        ks = pl.multiple_of(j * tk, tk)

        def scores_of(blk):
            kt = kc_ref[0, 0, blk, :].astype(BF16)
            return lambda qm: _dot(qm, kt)

        def pv_of(blk):
            vt = vc_ref[0, 0, blk, :].astype(BF16)
            return lambda p: _dot_nt(p, vt)

        head_loop(scores_of, pv_of,
                  lambda h: ccol_ref[0, :, h:h + 1] - crow_ref[0, h:h + 1, pl.ds(ks, tk)], None)

    cache_tile()

    @pl.when(j == n_kv - 1)
    def _():
        past = n_kv * tk
        q_pos = lax.broadcasted_iota(jnp.int32, (tq, tq), 0)
        k_pos = lax.broadcasted_iota(jnp.int32, (tq, tq), 1)
        head_loop(lambda blk: (lambda qm: _dot_nt(qm, kn_ref[:, blk])),
                  lambda blk: (lambda p: _dot(p, vn_ref[:, blk])),
                  lambda h: ccol_ref[0, :, h:h + 1] - crow_ref[0, h:h + 1, past:past + tq],
                  k_pos <= q_pos)
        for pair in range(n_heads // 2):
            blk = slice(pair * LANES, (pair + 1) * LANES)
            o0 = acc_ref[2 * pair] / l_ref[2 * pair]
            o1 = acc_ref[2 * pair + 1] / l_ref[2 * pair + 1]
            o_ref[:, blk] = (o0 * lo + o1 * hi).astype(BF16)


def _fox_decode(q, kt_cache, vt_cache, k_new, v_new, c_col, c_row, *, layer, tq, tk, n_heads):
    _, batch, width, past = kt_cache.shape
    n_kv = past // tk
    new_blk = pl.BlockSpec((tq, width), lambda b, j: (b, 0))
    cache_blk = pl.BlockSpec((1, 1, width, tk),
                             lambda b, j: (layer, b, 0, j))
    return pl.pallas_call(
        functools.partial(_fox_decode_kernel, tq=tq, tk=tk, n_kv=n_kv, n_heads=n_heads),
        grid=(batch, n_kv),
        in_specs=[new_blk, cache_blk, cache_blk, new_blk, new_blk,
                  pl.BlockSpec((1, tq, n_heads), lambda b, j: (b, past // tq, 0)),
                  pl.BlockSpec((1, n_heads, c_row.shape[2]), lambda b, j: (b, 0, 0))],
        out_specs=new_blk,
        out_shape=jax.ShapeDtypeStruct((batch * tq, width), BF16),
        scratch_shapes=[pltpu.VMEM((n_heads, tq, 1), F32), pltpu.VMEM((n_heads, tq, 1), F32),
                        pltpu.VMEM((n_heads, tq, LANES), F32)],
        compiler_params=_cparams(2),
        name="fox_decode",
    )(q, kt_cache, vt_cache, k_new, v_new, c_col, c_row)


def _diff_decode_kernel(q_ref, kc_ref, vc_ref, kn_ref, vn_ref, lq1_ref, lk1_ref, lq2_ref, lk2_ref,
                        g_ref, o_ref, m_ref, l_ref, acc_ref,
                        *, tq, tk, n_kv, n_heads, lambda_init):
    j = pl.program_id(1)
    lo, hi = _lane_half_masks(tq)
    past = n_kv * tk
    q_abs = (past + lax.broadcasted_iota(jnp.int32, (tq, 1), 0)).astype(F32)

    @pl.when(j == 0)
    def _():
        _flash_init(m_ref, l_ref, acc_ref)

    def head_loop(scores_of, pv_of, dist):
        pending = []
        for h in range(n_heads):
            blk = slice(h * LANES, (h + 1) * LANES)
            slope = _alibi_slope(h, n_heads)
            q2 = q_ref[:, blk]
            scores = scores_of(blk)
            pv = pv_of(h)
            for hm in (lo, hi):
                pending.append((scores(_masked_q(q2, hm)) - slope * dist, pv))
        for idx, (s, pv) in enumerate(pending):
            _flash_step(s, pv, m_ref.at[idx], l_ref.at[idx], acc_ref.at[idx])

    def cache_tile():
        k_abs = (j * tk + lax.broadcasted_iota(jnp.int32, (1, tk), 1)).astype(F32)

        def scores_of(blk):
            kt = kc_ref[0, 0, blk, :].astype(BF16)
            return lambda qm: _dot(qm, kt)

        def pv_of(h):
            v2 = vc_ref[0, 0, pl.ds(h, tk, stride=n_heads), :].astype(BF16)
            return lambda p: _dot(p, v2)

        head_loop(scores_of, pv_of, jnp.abs(q_abs - k_abs))

    cache_tile()

    @pl.when(j == n_kv - 1)
    def _():
        assert past % CHUNK == 0 and tq <= CHUNK
        k_abs = (past + lax.broadcasted_iota(jnp.int32, (1, tq), 1)).astype(F32)
        head_loop(lambda blk: (lambda qm: _dot_nt(qm, kn_ref[:, blk])),
                  lambda h: (lambda p: _dot(p, vn_ref[:, h * LANES:(h + 1) * LANES])),
                  jnp.abs(q_abs - k_abs))
        lam = _lambda_scalar(lq1_ref[...], lk1_ref[...], lq2_ref[...], lk2_ref[...], lambda_init)
        for h in range(n_heads):
            blk = slice(h * LANES, (h + 1) * LANES)
            o = acc_ref[2 * h] / l_ref[2 * h] - lam * (acc_ref[2 * h + 1] / l_ref[2 * h + 1])
            o_ref[:, blk] = _head_rms(o, g_ref[:, blk], lambda_init).astype(BF16)


def _diff_decode(q, kt_cache, v_cache, k_new, v_new, lams, g, *, layer, tq, tk, n_heads,
                 lambda_init):
    _, batch, width, past = kt_cache.shape
    n_kv = past // tk
    new_blk = pl.BlockSpec((tq, width), lambda b, j: (b, 0))
    return pl.pallas_call(
        functools.partial(_diff_decode_kernel, tq=tq, tk=tk, n_kv=n_kv, n_heads=n_heads,
                          lambda_init=lambda_init),
        grid=(batch, n_kv),
        in_specs=[new_blk,
                  pl.BlockSpec((1, 1, width, tk),
                               lambda b, j: (layer, b, 0, j)),
                  pl.BlockSpec((1, 1, tk * n_heads, LANES),
                               lambda b, j: (layer, b, j, 0)),
                  new_blk, new_blk]
                 + [_const_spec(a.shape) for a in lams] + [_const_spec(g.shape)],
        out_specs=new_blk,
        out_shape=jax.ShapeDtypeStruct((batch * tq, width), BF16),
        scratch_shapes=[pltpu.VMEM((2 * n_heads, tq, 1), F32),
                        pltpu.VMEM((2 * n_heads, tq, 1), F32),
                        pltpu.VMEM((2 * n_heads, tq, LANES), F32)],
        compiler_params=_cparams(2),
        name="diff_decode",
    )(q, kt_cache, v_cache, k_new, v_new, *lams, g)


def _post_kernel(x_ref, oa_ref, ob_ref, wgt_ref, wba_ref, wbb_ref, wo_ref, g_ref, b_ref, o_ref,
                 *, alpha):
    x = x_ref[...]
    xb = x.astype(BF16)
    d = x.shape[1]
    merged = _sigmoid(_dot_nt(xb, wgt_ref[:d, :])) * _dot(oa_ref[...], wba_ref[...])
    merged = merged + _sigmoid(_dot_nt(xb, wgt_ref[d:, :])) * _dot(ob_ref[...], wbb_ref[...])
    y = alpha * x + _dot(merged.astype(BF16), wo_ref[...])
    o_ref[...] = _layer_norm(y, g_ref[...], b_ref[...])


def _post(x, oa, ob, wt_gate, w_ba, w_bb, w_o, g, b, *, layer, tm, alpha):
    m, d = x.shape
    row = lambda i: (i, 0)
    return pl.pallas_call(
        functools.partial(_post_kernel, alpha=alpha),
        grid=(m // tm,),
        in_specs=[pl.BlockSpec((tm, d), row), pl.BlockSpec((tm, oa.shape[1]), row),
                  pl.BlockSpec((tm, ob.shape[1]), row), _const_spec(wt_gate.shape)]
                 + [_layer_spec(a.shape, layer) for a in (w_ba, w_bb, w_o, g, b)],
        out_specs=pl.BlockSpec((tm, d), row),
        out_shape=jax.ShapeDtypeStruct((m, d), F32),
        compiler_params=_cparams(1),
        name="merge_outproj_ln",
    )(x, oa, ob, wt_gate, w_ba, w_bb, w_o, g, b)


def _ffn_kernel(x_ref, p_ref, wg_ref, wu_ref, wd_ref, wpg_ref, wpp_ref, g_ref, b_ref, o_ref,
                *, alpha):
    x = x_ref[...]
    xb = x.astype(BF16)
    hg = _dot(xb, wg_ref[...])
    hidden = (hg * _sigmoid(hg) * _dot(xb, wu_ref[...])).astype(BF16)
    y = alpha * x + _dot(hidden, wd_ref[...])
    ple = _sigmoid(_dot(xb, wpg_ref[...])) * _dot(p_ref[...].astype(BF16), wpp_ref[...])
    o_ref[...] = _layer_norm(y + ple, g_ref[...], b_ref[...])


def _ffn(x, p, w_g, w_u, w_d, w_pg, w_pp, g, b, *, layer, tm, alpha):
    m, d = x.shape
    row = lambda i: (i, 0)
    return pl.pallas_call(
        functools.partial(_ffn_kernel, alpha=alpha),
        grid=(m // tm,),
        in_specs=[pl.BlockSpec((tm, d), row),
                  pl.BlockSpec((None, tm, p.shape[2]), lambda i: (layer, i, 0))]
                 + [_layer_spec(a.shape, layer) for a in (w_g, w_u, w_d, w_pg, w_pp, g, b)],
        out_specs=pl.BlockSpec((tm, d), row),
        out_shape=jax.ShapeDtypeStruct((m, d), F32),
        compiler_params=_cparams(1),
        name="swiglu_ple_ln",
    )(x, p, w_g, w_u, w_d, w_pg, w_pp, g, b)


def _row_tile(m):
    for tm in (512, 256, 128):
        if m % tm == 0:
            return tm
    raise ValueError(f"row count {m} is not a multiple of 128")


def kernel(x_prompt, x_sample, p_prompt, p_sample, cache_fox_k, cache_fox_v, cache_fox_logf, cache_diff_k, cache_diff_v, w_in, b_forget, lambda_q1, lambda_k1, lambda_q2, lambda_k2, diff_norm_g, w_branch_fox, w_branch_diff, w_out, ln1_g, ln1_b, w_ffn_gate, w_ffn_up, w_ffn_down, w_ple_gate, w_ple_proj, ln2_g, ln2_b):
    batch, seq, d_model = x_prompt.shape
    dec_batch, dec_seq, _ = x_sample.shape
    depth = w_in.shape[0]
    past = cache_fox_k.shape[2]
    h_a = cache_fox_k.shape[3]
    h_b = cache_diff_k.shape[3]
    w_a = h_a * HD_A
    w_b = h_b * 2 * DH_B
    assert w_a == w_b and w_in.shape[2] == 3 * w_a + h_a + 3 * w_b + 2 * d_model
    assert HD_A == DH_B
    alpha = (2 * depth) ** 0.25
    m_p = batch * seq
    m_s = dec_batch * dec_seq
    tm_p = _row_tile(seq)
    tm_s = _row_tile(m_s)
    tq, tk = min(512, seq), min(256, seq)
    tk_dec = min(2048, past)
    assert seq % tq == 0 and tq % tk == 0 and tk % CHUNK == 0 and past % tk_dec == 0
    t_dec = past + dec_seq
    t_dec_pad = -(-t_dec // SCAN_BLOCK) * SCAN_BLOCK
    qk_scale = HD_A ** -0.5
    f_pad = 16

    xp = x_prompt.reshape(m_p, d_model)
    xs = x_sample.reshape(m_s, d_model)

    wt_in = jnp.swapaxes(w_in, 1, 2)
    kt_cache_a = jnp.transpose(cache_fox_k, (0, 1, 3, 4, 2)).reshape(depth, dec_batch, w_a, past)
    vt_cache_a = jnp.transpose(cache_fox_v, (0, 1, 3, 4, 2)).reshape(depth, dec_batch, w_a, past)
    kt_cache_b = jnp.transpose(cache_diff_k, (0, 1, 3, 4, 5, 2)).reshape(depth, dec_batch, w_b, past)
    lf_cache = jnp.swapaxes(cache_fox_logf, 2, 3).astype(F32)
    v_cache_b = cache_diff_v.reshape(depth, dec_batch, past * h_b, 2 * DH_B)

    post_w = (w_branch_fox.astype(BF16), w_branch_diff.astype(BF16), w_out.astype(BF16),
              ln1_g.reshape(depth, 1, d_model), ln1_b.reshape(depth, 1, d_model))
    ffn_w = (w_ffn_gate.astype(BF16), w_ffn_up.astype(BF16), w_ffn_down.astype(BF16),
             w_ple_gate.astype(BF16), w_ple_proj.astype(BF16),
             ln2_g.reshape(depth, 1, d_model), ln2_b.reshape(depth, 1, d_model))
    pp = p_prompt.reshape(depth, m_p, -1)
    ps = p_sample.reshape(depth, m_s, -1)

    o_f = 3 * w_a
    o_qb = o_f + h_a
    o_ga = o_qb + 3 * w_b

    stacked = (jnp.zeros((depth, batch, w_a, seq), F32), jnp.zeros((depth, batch, w_a, seq), F32),
               jnp.zeros((depth, batch, w_b, seq), F32),
               jnp.zeros((depth, batch, seq, h_b, 2 * DH_B), F32),
               jnp.zeros((depth, batch, h_a, seq), F32))
    new_s = [[] for _ in range(5)]
    for l in range(depth):
        lambda_init = 0.8 - 0.6 * math.exp(-0.3 * l)
        wt = wt_in[l]
        wt_qkv = jnp.concatenate([wt[:o_f], wt[o_qb:o_ga]], axis=0).astype(BF16)
        wt_f = jnp.pad(wt[o_f:o_qb], ((0, f_pad - h_a), (0, 0))).astype(BF16)
        b_f = jnp.pad(b_forget[l], (0, f_pad - h_a))
        wt_gate = wt[o_ga:].astype(BF16)

        (qat, ka, kat, vat, vat_b, qbt, kb, kbt, vb, vbt_b, logf_p) = _inproj_prompt(
            xp, wt_qkv, wt_f, b_f.reshape(f_pad, 1), stacked, layer=l, depth=depth, batch=batch,
            seq=seq, tm=tm_p, n_heads=h_a, q_scale=qk_scale * LOG2E)
        stacked = (kat, vat, kbt, vb, logf_p)
        (qa_s, qb_s, ka_s, va_s, kb_s, vb_s, ka_sb, va_sb, kb_sb, vb_sb, logf_s) = _inproj_decode(
            xs, wt_qkv, wt_f, b_f.reshape(1, f_pad), n_heads=h_a, q_scale=qk_scale)

        c_row_p, aug_p = _cumsum_aug(logf_p[l], LOG2E, width=w_a)
        lf_s = jnp.swapaxes(logf_s.reshape(dec_batch, dec_seq, h_a), 1, 2)
        lf_all = jnp.concatenate([lf_cache[l], lf_s], axis=2)
        lf_all = jnp.pad(lf_all, ((0, 0), (0, 0), (0, t_dec_pad - t_dec)))
        c_row_s = _cumsum_rows(lf_all.reshape(dec_batch * h_a, t_dec_pad), 1.0)
        c_row_s = c_row_s.reshape(dec_batch, h_a, t_dec_pad)
        c_col_s = jnp.swapaxes(c_row_s, 1, 2)

        lams = [a[l].reshape(1, DH_B) for a in (lambda_q1, lambda_k1, lambda_q2, lambda_k2)]
        g_diff = diff_norm_g[l].reshape(1, w_b)

        oa_p = _fox_prompt(qat, ka, vat_b, c_row_p, aug_p,
                           batch=batch, seq=seq, tq=tq, tk=tk, n_heads=h_a)
        ob_p = _diff_prompt(qbt, kb, vbt_b, lams, g_diff, batch=batch, seq=seq, tq=tq, tk=tk,
                            n_heads=h_b, lambda_init=lambda_init)
        oa_s = _fox_decode(qa_s, kt_cache_a, vt_cache_a, ka_sb, va_sb, c_col_s, c_row_s,
                           layer=l, tq=dec_seq, tk=tk_dec, n_heads=h_a)
        ob_s = _diff_decode(qb_s, kt_cache_b, v_cache_b, kb_sb, vb_sb, lams, g_diff,
                            layer=l, tq=dec_seq, tk=tk_dec, n_heads=h_b, lambda_init=lambda_init)

        xp = _post(xp, oa_p, ob_p, wt_gate, *post_w, layer=l, tm=tm_p, alpha=alpha)
        xp = _ffn(xp, pp, *ffn_w, layer=l, tm=tm_p, alpha=alpha)
        xs = _post(xs, oa_s, ob_s, wt_gate, *post_w, layer=l, tm=tm_s, alpha=alpha)
        xs = _ffn(xs, ps, *ffn_w, layer=l, tm=tm_s, alpha=alpha)

        for lst, r in zip(new_s, (ka_s, va_s, logf_s, kb_s, vb_s)):
            lst.append(r)

    kat, vat, kbt, vb, logf_p = stacked
    ka_s, va_s, logf_s, kb_s, vb_s = [jnp.stack(a) for a in new_s]
    fox_k_p = jnp.transpose(kat.reshape(depth, batch, h_a, HD_A, seq), (0, 1, 4, 2, 3))
    fox_v_p = jnp.transpose(vat.reshape(depth, batch, h_a, HD_A, seq), (0, 1, 4, 2, 3))
    fox_lf_p = jnp.swapaxes(logf_p, 2, 3)
    diff_k_p = jnp.transpose(kbt.reshape(depth, batch, h_b, 2, DH_B, seq), (0, 1, 5, 2, 3, 4))
    return (xp.reshape(batch, seq, d_model), xs.reshape(dec_batch, dec_seq, d_model),
            fox_k_p, fox_v_p, fox_lf_p, diff_k_p, vb,
            ka_s.reshape(depth, dec_batch, dec_seq, h_a, HD_A),
            va_s.reshape(depth, dec_batch, dec_seq, h_a, HD_A),
            logf_s.reshape(depth, dec_batch, dec_seq, h_a),
            kb_s.reshape(depth, dec_batch, dec_seq, h_b, 2, DH_B),
            vb_s.reshape(depth, dec_batch, dec_seq, h_b, 2 * DH_B))
```

```python
import functools
import math

import jax
import jax.numpy as jnp
import numpy as np
from jax import lax
from jax.experimental import pallas as pl
from jax.experimental.pallas import tpu as pltpu

F32 = jnp.float32
BF16 = jnp.bfloat16

HD_A = 64
DH_B = 64
CHUNK = 64
CHUNK_SHIFT = 6
LN_EPS = 1e-5
RMS_EPS = 1e-5
NEG_INF = -1e30
LOG2E = math.log2(math.e)
LANES = 128
VMEM_LIMIT = 56 * 1024 * 1024


def _cparams(n_axes):
    return pltpu.CompilerParams(dimension_semantics=("arbitrary",) * n_axes,
                                vmem_limit_bytes=VMEM_LIMIT)


def _const_spec(shape):
    zeros = (0,) * len(shape)
    return pl.BlockSpec(shape, lambda *_: zeros, pipeline_mode=pl.Buffered(1))


def _layer_spec(shape, layer):
    zeros = (0,) * (len(shape) - 1)
    return pl.BlockSpec((None,) + tuple(shape[1:]), lambda *_: (layer,) + zeros,
                        pipeline_mode=pl.Buffered(1))


def _dot(a, b):
    return jnp.dot(a, b, preferred_element_type=F32)


def _dot_nt(a, b):
    return lax.dot_general(a, b, (((1,), (1,)), ((), ())), preferred_element_type=F32)


def _sigmoid(x):
    return 1.0 / (1.0 + jnp.exp(-x))


def _log_sigmoid(z):
    return jnp.minimum(z, 0.0) - jnp.log1p(jnp.exp(-jnp.abs(z)))


def _layer_norm(y, g, b):
    mu = jnp.mean(y, axis=-1, keepdims=True)
    yc = y - mu
    var = jnp.mean(yc * yc, axis=-1, keepdims=True)
    return yc * lax.rsqrt(var + LN_EPS) * g + b


def _inproj_prompt_kernel(x_ref, wt_ref, wft_ref, bf_ref, *refs, width, n_heads, q_scale):
    (qat_ref, ka_ref, kat_ref, vat_ref, vatb_ref,
     qbt_ref, kb_ref, kbt_ref, vb_ref, vbtb_ref, logf_ref) = refs[-11:]
    xb = x_ref[...].astype(BF16)

    def proj_t(i):
        return _dot_nt(wt_ref[i * width:(i + 1) * width, :], xb)

    qat_ref[0] = (proj_t(0) * q_scale).astype(BF16)
    kat = proj_t(1)
    kat_ref[0, 0] = kat
    ka_ref[...] = kat.T.astype(BF16)
    vat = proj_t(2)
    vat_ref[0, 0] = vat
    vatb_ref[0] = vat.astype(BF16)
    qbt_ref[0] = (proj_t(3) * q_scale).astype(BF16)
    kbt = proj_t(4)
    kbt_ref[0, 0] = kbt
    kb_ref[...] = kbt.T.astype(BF16)
    vbt = proj_t(5)
    vbtb_ref[0] = vbt.astype(BF16)
    vb = vbt.T
    for h in range(width // LANES):
        vb_ref[0, 0, :, h, :] = vb[:, h * LANES:(h + 1) * LANES]
    z = _dot_nt(wft_ref[...], xb) + bf_ref[...]
    logf_ref[0, 0] = _log_sigmoid(z)[:n_heads, :]


def _inproj_prompt(x, wt_qkv, wt_f, b_f_col, stacked, *, layer, depth, batch, seq, tm, n_heads,
                   q_scale):
    m, d = x.shape
    width = wt_qkv.shape[0] // 6
    nt = seq // tm
    row = lambda i: (i, 0)
    tr = lambda i: (i // nt, 0, i % nt)
    tr_l = lambda i: (layer, i // nt, 0, i % nt)
    t_blk = pl.BlockSpec((1, width, tm), tr)
    t_blk_l = pl.BlockSpec((1, 1, width, tm), tr_l)
    n_blk = pl.BlockSpec((tm, width), row)
    t_f32 = jax.ShapeDtypeStruct((depth, batch, width, seq), F32)
    t_b16 = jax.ShapeDtypeStruct((batch, width, seq), BF16)
    n_b16 = jax.ShapeDtypeStruct((m, width), BF16)
    vb_heads = width // LANES
    out_specs = [t_blk, n_blk, t_blk_l, t_blk_l, t_blk, t_blk, n_blk, t_blk_l,
                 pl.BlockSpec((1, 1, tm, vb_heads, LANES),
                              lambda i: (layer, i // nt, i % nt, 0, 0)),
                 t_blk, pl.BlockSpec((1, 1, n_heads, tm), tr_l)]
    out_shape = [t_b16, n_b16, t_f32, t_f32, t_b16, t_b16, n_b16, t_f32,
                 jax.ShapeDtypeStruct((depth, batch, seq, vb_heads, LANES), F32),
                 t_b16, jax.ShapeDtypeStruct((depth, batch, n_heads, seq), F32)]
    stacked_outputs = (2, 3, 7, 8, 10)
    in_specs = [pl.BlockSpec((tm, d), row), _const_spec(wt_qkv.shape), _const_spec(wt_f.shape),
                _const_spec(b_f_col.shape)]
    args = [x, wt_qkv, wt_f, b_f_col]
    aliases = {len(args) + i: o for i, o in enumerate(stacked_outputs)}
    in_specs += [pl.BlockSpec(memory_space=pl.ANY)] * len(stacked)
    args += list(stacked)
    return pl.pallas_call(
        functools.partial(_inproj_prompt_kernel, width=width, n_heads=n_heads, q_scale=q_scale),
        grid=(m // tm,),
        in_specs=in_specs,
        out_specs=out_specs,
        out_shape=out_shape,
        input_output_aliases=aliases,
        compiler_params=_cparams(1),
        name="inproj_prompt",
    )(*args)


def _inproj_decode_kernel(x_ref, wt_ref, wft_ref, bf_ref,
                          qa_ref, qb_ref, ka_ref, va_ref, kb_ref, vb_ref,
                          kab_ref, vab_ref, kbb_ref, vbb_ref, logf_ref, *, width, n_heads, q_scale):
    xb = x_ref[...].astype(BF16)

    def proj(i):
        return _dot_nt(xb, wt_ref[i * width:(i + 1) * width, :])

    qa_ref[...] = (proj(0) * q_scale).astype(BF16)
    qb_ref[...] = (proj(3) * q_scale).astype(BF16)
    for i, (full_ref, half_ref) in ((1, (ka_ref, kab_ref)), (2, (va_ref, vab_ref)),
                                    (4, (kb_ref, kbb_ref)), (5, (vb_ref, vbb_ref))):
        val = proj(i)
        full_ref[...] = val
        half_ref[...] = val.astype(BF16)
    z = _dot_nt(xb, wft_ref[...]) + bf_ref[...]
    logf_ref[...] = _log_sigmoid(z)[:, :n_heads]


def _inproj_decode(x, wt_qkv, wt_f, b_f_row, *, n_heads, q_scale):
    m, d = x.shape
    width = wt_qkv.shape[0] // 6
    out_shape = ([jax.ShapeDtypeStruct((m, width), BF16)] * 2
                 + [jax.ShapeDtypeStruct((m, width), F32)] * 4
                 + [jax.ShapeDtypeStruct((m, width), BF16)] * 4
                 + [jax.ShapeDtypeStruct((m, n_heads), F32)])
    return pl.pallas_call(
        functools.partial(_inproj_decode_kernel, width=width, n_heads=n_heads, q_scale=q_scale),
        out_shape=out_shape,
        compiler_params=pltpu.CompilerParams(vmem_limit_bytes=VMEM_LIMIT),
        name="inproj_decode",
    )(x, wt_qkv, wt_f, b_f_row)


SCAN_BLOCK = 256


def _split3(a):
    a1 = a.astype(BF16)
    r1 = a - a1.astype(F32)
    a2 = r1.astype(BF16)
    a3 = (r1 - a2.astype(F32)).astype(BF16)
    return a1, a2, a3


def _split3_const(x):
    pieces, rest = [], np.float32(x)
    for _ in range(3):
        p = np.float32(np.asarray(rest, dtype=jnp.bfloat16))
        pieces.append(float(p))
        rest = np.float32(rest - p)
    return pieces


def _cumsum_kernel(lf_ref, c_ref, *, n_blk, scale):
    rows = lf_ref.shape[0]
    r = lax.broadcasted_iota(jnp.int32, (SCAN_BLOCK, SCAN_BLOCK), 0)
    c = lax.broadcasted_iota(jnp.int32, (SCAN_BLOCK, SCAN_BLOCK), 1)
    upper = jnp.where(r <= c, 1.0, 0.0).astype(BF16)
    carry = jnp.zeros((rows, 1), F32)
    for j in range(n_blk):
        sl = slice(j * SCAN_BLOCK, (j + 1) * SCAN_BLOCK)
        a1, a2, a3 = _split3(lf_ref[:, sl])
        blk = _dot(a1, upper) + _dot(a2, upper) + _dot(a3, upper) + carry
        c_ref[:, sl] = blk * scale
        carry = blk[:, SCAN_BLOCK - 1:SCAN_BLOCK]


def _cumsum_rows(lf_rows, scale):
    rows, t = lf_rows.shape
    assert t % SCAN_BLOCK == 0
    return pl.pallas_call(
        functools.partial(_cumsum_kernel, n_blk=t // SCAN_BLOCK, scale=scale),
        out_shape=jax.ShapeDtypeStruct((rows, t), F32),
        compiler_params=pltpu.CompilerParams(vmem_limit_bytes=VMEM_LIMIT),
        name="logf_cumsum",
    )(lf_rows)


AUG_PIECES = 3


def _aug_lane(h, piece):
    return AUG_PIECES * (h % 2) + piece


def _cumsum_aug_kernel(lf_rows_ref, lf_cols_ref, place_ref, c_ref, aug_ref, *, n_blk, scale):
    rows = lf_rows_ref.shape[0]
    batch, _, width = aug_ref.shape
    r = lax.broadcasted_iota(jnp.int32, (SCAN_BLOCK, SCAN_BLOCK), 0)
    c = lax.broadcasted_iota(jnp.int32, (SCAN_BLOCK, SCAN_BLOCK), 1)
    upper = jnp.where(r <= c, 1.0, 0.0).astype(BF16)
    lower = jnp.where(c <= r, 1.0, 0.0).astype(BF16)
    carry_r = jnp.zeros((rows, 1), F32)
    carry_c = jnp.zeros((1, rows), F32)
    for j in range(n_blk):
        sl = slice(j * SCAN_BLOCK, (j + 1) * SCAN_BLOCK)
        a1, a2, a3 = _split3(lf_rows_ref[:, sl])
        blk_r = _dot(a1, upper) + _dot(a2, upper) + _dot(a3, upper) + carry_r
        c_ref[:, sl] = blk_r * scale
        carry_r = blk_r[:, SCAN_BLOCK - 1:SCAN_BLOCK]
        b1, b2, b3 = _split3(lf_cols_ref[sl, :])
        blk_c = _dot(lower, b1) + _dot(lower, b2) + _dot(lower, b3) + carry_c
        carry_c = blk_c[SCAN_BLOCK - 1:SCAN_BLOCK, :]
        pieces = jnp.concatenate(_split3(blk_c * scale), axis=1)
        placed = _dot(pieces, place_ref[...]).astype(BF16)
        for b in range(batch):
            aug_ref[b, sl, :] = placed[:, b * width:(b + 1) * width]


def _placement_matrix(batch, n_heads, width):
    rows = batch * n_heads
    place = np.zeros((AUG_PIECES * rows, batch * width), np.float32)
    for b in range(batch):
        for h in range(n_heads):
            for p in range(AUG_PIECES):
                place[p * rows + b * n_heads + h,
                      b * width + (h // 2) * LANES + _aug_lane(h, p)] = 1.0
    return jnp.asarray(place, BF16)


def _cumsum_aug(lf, scale, *, width):
    batch, n_heads, t = lf.shape
    assert t % SCAN_BLOCK == 0
    rows = batch * n_heads
    c_rows, aug = pl.pallas_call(
        functools.partial(_cumsum_aug_kernel, n_blk=t // SCAN_BLOCK, scale=scale),
        out_shape=[jax.ShapeDtypeStruct((rows, t), F32),
                   jax.ShapeDtypeStruct((batch, t, width), BF16)],
        compiler_params=pltpu.CompilerParams(vmem_limit_bytes=VMEM_LIMIT),
        name="logf_cumsum_aug",
    )(lf.reshape(rows, t), jnp.transpose(lf, (2, 0, 1)).reshape(t, rows),
      _placement_matrix(batch, n_heads, width))
    return c_rows.reshape(batch, n_heads, t), aug


def _lambda_scalar(lq1, lk1, lq2, lk2, lambda_init):
    return (jnp.exp(jnp.sum(lq1 * lk1, axis=1, keepdims=True))
            - jnp.exp(jnp.sum(lq2 * lk2, axis=1, keepdims=True)) + lambda_init)


def _head_rms(o, g_row, lambda_init):
    o = o * lax.rsqrt(jnp.mean(o * o, axis=-1, keepdims=True) + RMS_EPS)
    return o * g_row * (1.0 - lambda_init)


def _alibi_slope(h, n_heads):
    return 2.0 ** (-8.0 * (h + 1) / n_heads)


def _sublane_half_masks(cols):
    row = lax.broadcasted_iota(jnp.int32, (LANES, cols), 0)
    lo = jnp.where(row < HD_A, 1.0, 0.0)
    return lo, 1.0 - lo


SUM_ROWS = 16


def _with_sum_rows(v_t):
    return jnp.concatenate([v_t, jnp.ones((SUM_ROWS, v_t.shape[1]), BF16)], axis=0)


def _flash_step_refs(i, t, shift, v_aug, m_ref, acc_ref, qs=slice(None)):
    m = m_ref[i, :, qs]
    m_new = jnp.maximum(m, jnp.max(t, axis=0, keepdims=True) + shift)
    p = jnp.exp2(t - (m_new - shift)).astype(BF16)
    acc_ref[i, :, qs] = jnp.exp2(m - m_new) * acc_ref[i, :, qs] + _dot(v_aug, p)
    m_ref[i, :, qs] = m_new


def _flash_init_refs(m_ref, acc_ref):
    m_ref[...] = jnp.full(m_ref.shape, NEG_INF, F32)
    acc_ref[...] = jnp.zeros(acc_ref.shape, F32)


def _flash_result(i, d, acc_ref):
    acc = acc_ref[i]
    return acc[:d] / acc[d:d + 1]


def _stream_scratch(n_streams, d, tq):
    return [pltpu.VMEM((n_streams, 2 * LANES, tq), BF16), pltpu.VMEM((n_streams, 1, tq), F32),
            pltpu.VMEM((n_streams, d + SUM_ROWS, tq), F32)]


def _fox_prompt_kernel(qt_ref, k_ref, vt_ref, crow_ref, aug_ref, o_ref, qm_ref, m_ref, acc_ref,
                       *, tq, tk, n_heads):
    qi = pl.program_id(1)
    n_diag = tq // tk
    lo, hi = _sublane_half_masks(tq)
    row = lax.broadcasted_iota(jnp.int32, (LANES, tq), 0)
    for h in range(n_heads):
        blk = slice((h // 2) * LANES, (h // 2 + 1) * LANES)
        qm_ref[h, :LANES] = (qt_ref[0, blk, :].astype(F32) * (lo, hi)[h % 2]).astype(BF16)
        picks = (row >= _aug_lane(h, 0)) & (row < _aug_lane(h, AUG_PIECES))
        qm_ref[h, LANES:] = jnp.where(picks, -1.0, 0.0).astype(BF16)
    _flash_init_refs(m_ref, acc_ref)

    def tile(ks, mask, qs=slice(None)):
        v_all = vt_ref[0, :, pl.ds(ks, tk)]
        scores = []
        for h in range(n_heads):
            blk = slice((h // 2) * LANES, (h // 2 + 1) * LANES)
            k_aug = jnp.concatenate([k_ref[pl.ds(ks, tk), blk], aug_ref[0, pl.ds(ks, tk), blk]],
                                    axis=1)
            t = _dot(k_aug, qm_ref[h, :, qs])
            if mask is not None:
                t = jnp.where(mask, t, NEG_INF)
            scores.append(t)
        for h in range(n_heads):
            _flash_step_refs(h, scores[h], crow_ref[0, h:h + 1, qs],
                             _with_sum_rows(v_all[h * HD_A:(h + 1) * HD_A]), m_ref, acc_ref, qs)

    def full_tile(j, carry):
        tile(pl.multiple_of(j * tk, tk), None)
        return carry

    lax.fori_loop(0, qi * n_diag, full_tile, 0)
    for d in range(n_diag):
        nq = tq - d * tk
        r = lax.broadcasted_iota(jnp.int32, (tk, nq), 0)
        c = lax.broadcasted_iota(jnp.int32, (tk, nq), 1)
        tile(pl.multiple_of(qi * tq + d * tk, tk), r <= c, slice(d * tk, tq))
    for pair in range(n_heads // 2):
        blk = slice(pair * LANES, (pair + 1) * LANES)
        o_t = jnp.concatenate([_flash_result(2 * pair, HD_A, acc_ref),
                               _flash_result(2 * pair + 1, HD_A, acc_ref)], axis=0)
        o_ref[:, blk] = o_t.T.astype(BF16)


def _fox_prompt(qt, k, vt, c_row, aug, *, batch, seq, tq, tk, n_heads):
    width = n_heads * HD_A
    nq = seq // tq
    return pl.pallas_call(
        functools.partial(_fox_prompt_kernel, tq=tq, tk=tk, n_heads=n_heads),
        grid=(batch, nq),
        in_specs=[pl.BlockSpec((1, width, tq), lambda b, i: (b, 0, i)),
                  pl.BlockSpec((seq, width), lambda b, i: (b, 0)),
                  pl.BlockSpec((1, width, seq), lambda b, i: (b, 0, 0)),
                  pl.BlockSpec((1, n_heads, tq), lambda b, i: (b, 0, i)),
                  pl.BlockSpec((1, seq, width), lambda b, i: (b, 0, 0))],
        out_specs=pl.BlockSpec((tq, width), lambda b, i: (b * nq + i, 0)),
        out_shape=jax.ShapeDtypeStruct((batch * seq, width), BF16),
        scratch_shapes=_stream_scratch(n_heads, HD_A, tq),
        compiler_params=_cparams(2),
        name="fox_prompt",
    )(qt, k, vt, c_row, aug)


def _diff_prompt_kernel(qt_ref, k_ref, vt_ref, lq1_ref, lk1_ref, lq2_ref, lk2_ref, g_ref, o_ref,
                        qm_ref, m_ref, acc_ref, *, tq, tk, n_heads, lambda_init):
    qi = pl.program_id(1)
    n_diag = tq // tk
    lo, hi = _sublane_half_masks(tq)
    row = lax.broadcasted_iota(jnp.int32, (LANES, tq), 0)
    for i in range(2 * n_heads):
        blk = slice((i // 2) * LANES, (i // 2 + 1) * LANES)
        qm_ref[i, :LANES] = (qt_ref[0, blk, :].astype(F32) * (lo, hi)[i % 2]).astype(BF16)
        s1, s2, s3 = _split3_const(_alibi_slope(i // 2, n_heads) * LOG2E)
        slope_rows = jnp.where(row == 0, s1, jnp.where(row == 1, s2, jnp.where(row == 2, s3, 0.0)))
        qm_ref[i, LANES:] = slope_rows.astype(BF16)
    _flash_init_refs(m_ref, acc_ref)
    assert tk <= 256
    lane = lax.broadcasted_iota(jnp.int32, (tk, LANES), 1)
    key_off = lax.broadcasted_iota(jnp.int32, (tk, LANES), 0).astype(F32)
    key_off = jnp.where(lane < AUG_PIECES, key_off, 0.0).astype(BF16)
    q_off = lax.broadcasted_iota(jnp.int32, (1, tq), 1).astype(F32)

    def tile(ks, scores_of, shift_of, qs=slice(None)):
        k_all = k_ref[pl.ds(ks, tk), :]
        v_all = vt_ref[0, :, pl.ds(ks, tk)]
        scores = [scores_of(k_all[:, (i // 2) * LANES:(i // 2 + 1) * LANES], i)
                  for i in range(2 * n_heads)]
        for i in range(2 * n_heads):
            h = i // 2
            _flash_step_refs(i, scores[i], shift_of(_alibi_slope(h, n_heads) * LOG2E),
                             _with_sum_rows(v_all[h * LANES:(h + 1) * LANES]), m_ref, acc_ref, qs)

    def full_tile(j, carry):
        off = (j * tk - qi * tq).astype(F32) - q_off
        tile(pl.multiple_of(j * tk, tk),
             lambda k2, i: _dot(jnp.concatenate([k2, key_off], axis=1), qm_ref[i]),
             lambda s2: s2 * off)
        return carry

    lax.fori_loop(0, qi * n_diag, full_tile, 0)
    for d in range(n_diag):
        qs = slice(d * tk, tq)
        r = lax.broadcasted_iota(jnp.int32, (tk, tq - d * tk), 0)
        c = lax.broadcasted_iota(jnp.int32, (tk, tq - d * tk), 1)
        dist = jnp.abs(r - c).astype(F32)
        visible = (lax.shift_right_logical(r, CHUNK_SHIFT)
                   <= lax.shift_right_logical(c, CHUNK_SHIFT))

        def diag_scores(k2, i):
            slope2 = _alibi_slope(i // 2, n_heads) * LOG2E
            return jnp.where(visible, _dot(k2, qm_ref[i, :LANES, qs]) - slope2 * dist, NEG_INF)

        tile(pl.multiple_of(qi * tq + d * tk, tk), diag_scores, lambda s2: 0.0, qs)
    lam = _lambda_scalar(lq1_ref[...], lk1_ref[...], lq2_ref[...], lk2_ref[...], lambda_init)
    for h in range(n_heads):
        blk = slice(h * LANES, (h + 1) * LANES)
        o = (_flash_result(2 * h, LANES, acc_ref) - lam * _flash_result(2 * h + 1, LANES, acc_ref)).T
        o_ref[:, blk] = _head_rms(o, g_ref[:, blk], lambda_init).astype(BF16)


def _diff_prompt(qt, k, vt, lams, g, *, batch, seq, tq, tk, n_heads, lambda_init):
    width = n_heads * 2 * DH_B
    nq = seq // tq
    return pl.pallas_call(
        functools.partial(_diff_prompt_kernel, tq=tq, tk=tk, n_heads=n_heads,
                          lambda_init=lambda_init),
        grid=(batch, nq),
        in_specs=[pl.BlockSpec((1, width, tq), lambda b, i: (b, 0, i)),
                  pl.BlockSpec((seq, width), lambda b, i: (b, 0)),
                  pl.BlockSpec((1, width, seq), lambda b, i: (b, 0, 0))]
                 + [_const_spec(a.shape) for a in lams] + [_const_spec(g.shape)],
        out_specs=pl.BlockSpec((tq, width), lambda b, i: (b * nq + i, 0)),
        out_shape=jax.ShapeDtypeStruct((batch * seq, width), BF16),
        scratch_shapes=_stream_scratch(2 * n_heads, LANES, tq),
        compiler_params=_cparams(2),
        name="diff_prompt",
    )(qt, k, vt, *lams, g)


def _lane_half_masks(rows):
    lane = lax.broadcasted_iota(jnp.int32, (rows, LANES), 1)
    lo = jnp.where(lane < HD_A, 1.0, 0.0)
    return lo, 1.0 - lo


def _masked_q(q2, mask):
    return (q2.astype(F32) * mask).astype(BF16)


def _flash_init(m_ref, l_ref, acc_ref):
    m_ref[...] = jnp.full(m_ref.shape, NEG_INF, F32)
    l_ref[...] = jnp.zeros(l_ref.shape, F32)
    acc_ref[...] = jnp.zeros(acc_ref.shape, F32)


def _flash_step(s, pv, m_ref, l_ref, acc_ref):
    m_prev = m_ref[...]
    m_new = jnp.maximum(m_prev, jnp.max(s, axis=1, keepdims=True))
    alpha = jnp.exp(m_prev - m_new)
    p = jnp.exp(s - m_new)
    l_ref[...] = alpha * l_ref[...] + jnp.sum(p, axis=1, keepdims=True)
    acc_ref[...] = alpha * acc_ref[...] + pv(p.astype(BF16))
    m_ref[...] = m_new


def _fox_decode_kernel(q_ref, kc_ref, vc_ref, kn_ref, vn_ref, ccol_ref, crow_ref, o_ref,
                       m_ref, l_ref, acc_ref, *, tq, tk, n_kv, n_heads):
    j = pl.program_id(1)
    lo, hi = _lane_half_masks(tq)

    @pl.when(j == 0)
    def _():
        _flash_init(m_ref, l_ref, acc_ref)

    def head_loop(scores_of, pv_of, bias_of, mask):
        pending = []
        for pair in range(n_heads // 2):
            blk = slice(pair * LANES, (pair + 1) * LANES)
            q2 = q_ref[:, blk]
            scores = scores_of(blk)
            pv = pv_of(blk)
            for half, hm in enumerate((lo, hi)):
                s = scores(_masked_q(q2, hm)) + bias_of(2 * pair + half)
                if mask is not None:
                    s = jnp.where(mask, s, NEG_INF)
                pending.append((s, pv))
        for h, (s, pv) in enumerate(pending):
            _flash_step(s, pv, m_ref.at[h], l_ref.at[h], acc_ref.at[h])

    def cache_tile():
        ks = pl.multiple_of(j * tk, tk)

        def scores_of(blk):
            kt = kc_ref[0, 0, blk, :].astype(BF16)
            return lambda qm: _dot(qm, kt)

        def pv_of(blk):
            vt = vc_ref[0, 0, blk, :].astype(BF16)
            return lambda p: _dot_nt(p, vt)

        head_loop(scores_of, pv_of,
                  lambda h: ccol_ref[0, :, h:h + 1] - crow_ref[0, h:h + 1, pl.ds(ks, tk)], None)

    cache_tile()

    @pl.when(j == n_kv - 1)
    def _():
        past = n_kv * tk
        q_pos = lax.broadcasted_iota(jnp.int32, (tq, tq), 0)
        k_pos = lax.broadcasted_iota(jnp.int32, (tq, tq), 1)
        head_loop(lambda blk: (lambda qm: _dot_nt(qm, kn_ref[:, blk])),
                  lambda blk: (lambda p: _dot(p, vn_ref[:, blk])),
                  lambda h: ccol_ref[0, :, h:h + 1] - crow_ref[0, h:h + 1, past:past + tq],
                  k_pos <= q_pos)
        for pair in range(n_heads // 2):
            blk = slice(pair * LANES, (pair + 1) * LANES)
            o0 = acc_ref[2 * pair] / l_ref[2 * pair]
            o1 = acc_ref[2 * pair + 1] / l_ref[2 * pair + 1]
            o_ref[:, blk] = (o0 * lo + o1 * hi).astype(BF16)


def _fox_decode(q, kt_cache, vt_cache, k_new, v_new, c_col, c_row, *, layer, tq, tk, n_heads):
    _, batch, width, past = kt_cache.shape
    n_kv = past // tk
    new_blk = pl.BlockSpec((tq, width), lambda b, j: (b, 0))
    cache_blk = pl.BlockSpec((1, 1, width, tk),
                             lambda b, j: (layer, b, 0, j))
    return pl.pallas_call(
        functools.partial(_fox_decode_kernel, tq=tq, tk=tk, n_kv=n_kv, n_heads=n_heads),
        grid=(batch, n_kv),
        in_specs=[new_blk, cache_blk, cache_blk, new_blk, new_blk,
                  pl.BlockSpec((1, tq, n_heads), lambda b, j: (b, past // tq, 0)),
                  pl.BlockSpec((1, n_heads, c_row.shape[2]), lambda b, j: (b, 0, 0))],
        out_specs=new_blk,
        out_shape=jax.ShapeDtypeStruct((batch * tq, width), BF16),
        scratch_shapes=[pltpu.VMEM((n_heads, tq, 1), F32), pltpu.VMEM((n_heads, tq, 1), F32),
                        pltpu.VMEM((n_heads, tq, LANES), F32)],
        compiler_params=_cparams(2),
        name="fox_decode",
    )(q, kt_cache, vt_cache, k_new, v_new, c_col, c_row)


def _diff_decode_kernel(q_ref, kc_ref, vc_ref, kn_ref, vn_ref, lq1_ref, lk1_ref, lq2_ref, lk2_ref,
                        g_ref, o_ref, m_ref, l_ref, acc_ref,
                        *, tq, tk, n_kv, n_heads, lambda_init):
    j = pl.program_id(1)
    lo, hi = _lane_half_masks(tq)
    past = n_kv * tk
    q_abs = (past + lax.broadcasted_iota(jnp.int32, (tq, 1), 0)).astype(F32)

    @pl.when(j == 0)
    def _():
        _flash_init(m_ref, l_ref, acc_ref)

    def head_loop(scores_of, pv_of, dist):
        pending = []
        for h in range(n_heads):
            blk = slice(h * LANES, (h + 1) * LANES)
            slope = _alibi_slope(h, n_heads)
            q2 = q_ref[:, blk]
            scores = scores_of(blk)
            pv = pv_of(h)
            for hm in (lo, hi):
                pending.append((scores(_masked_q(q2, hm)) - slope * dist, pv))
        for idx, (s, pv) in enumerate(pending):
            _flash_step(s, pv, m_ref.at[idx], l_ref.at[idx], acc_ref.at[idx])

    def cache_tile():
        k_abs = (j * tk + lax.broadcasted_iota(jnp.int32, (1, tk), 1)).astype(F32)

        def scores_of(blk):
            kt = kc_ref[0, 0, blk, :].astype(BF16)
            return lambda qm: _dot(qm, kt)

        def pv_of(h):
            v2 = vc_ref[0, 0, pl.ds(h, tk, stride=n_heads), :].astype(BF16)
            return lambda p: _dot(p, v2)

        head_loop(scores_of, pv_of, jnp.abs(q_abs - k_abs))

    cache_tile()

    @pl.when(j == n_kv - 1)
    def _():
        assert past % CHUNK == 0 and tq <= CHUNK
        k_abs = (past + lax.broadcasted_iota(jnp.int32, (1, tq), 1)).astype(F32)
        head_loop(lambda blk: (lambda qm: _dot_nt(qm, kn_ref[:, blk])),
                  lambda h: (lambda p: _dot(p, vn_ref[:, h * LANES:(h + 1) * LANES])),
                  jnp.abs(q_abs - k_abs))
        lam = _lambda_scalar(lq1_ref[...], lk1_ref[...], lq2_ref[...], lk2_ref[...], lambda_init)
        for h in range(n_heads):
            blk = slice(h * LANES, (h + 1) * LANES)
            o = acc_ref[2 * h] / l_ref[2 * h] - lam * (acc_ref[2 * h + 1] / l_ref[2 * h + 1])
            o_ref[:, blk] = _head_rms(o, g_ref[:, blk], lambda_init).astype(BF16)


def _diff_decode(q, kt_cache, v_cache, k_new, v_new, lams, g, *, layer, tq, tk, n_heads,
                 lambda_init):
    _, batch, width, past = kt_cache.shape
    n_kv = past // tk
    new_blk = pl.BlockSpec((tq, width), lambda b, j: (b, 0))
    return pl.pallas_call(
        functools.partial(_diff_decode_kernel, tq=tq, tk=tk, n_kv=n_kv, n_heads=n_heads,
                          lambda_init=lambda_init),
        grid=(batch, n_kv),
        in_specs=[new_blk,
                  pl.BlockSpec((1, 1, width, tk),
                               lambda b, j: (layer, b, 0, j)),
                  pl.BlockSpec((1, 1, tk * n_heads, LANES),
                               lambda b, j: (layer, b, j, 0)),
                  new_blk, new_blk]
                 + [_const_spec(a.shape) for a in lams] + [_const_spec(g.shape)],
        out_specs=new_blk,
        out_shape=jax.ShapeDtypeStruct((batch * tq, width), BF16),
        scratch_shapes=[pltpu.VMEM((2 * n_heads, tq, 1), F32),
                        pltpu.VMEM((2 * n_heads, tq, 1), F32),
                        pltpu.VMEM((2 * n_heads, tq, LANES), F32)],
        compiler_params=_cparams(2),
        name="diff_decode",
    )(q, kt_cache, v_cache, k_new, v_new, *lams, g)


def _post_kernel(x_ref, oa_ref, ob_ref, wgt_ref, wba_ref, wbb_ref, wo_ref, g_ref, b_ref, o_ref,
                 *, alpha):
    x = x_ref[...]
    xb = x.astype(BF16)
    d = x.shape[1]
    merged = _sigmoid(_dot_nt(xb, wgt_ref[:d, :])) * _dot(oa_ref[...], wba_ref[...])
    merged = merged + _sigmoid(_dot_nt(xb, wgt_ref[d:, :])) * _dot(ob_ref[...], wbb_ref[...])
    y = alpha * x + _dot(merged.astype(BF16), wo_ref[...])
    o_ref[...] = _layer_norm(y, g_ref[...], b_ref[...])


def _post(x, oa, ob, wt_gate, w_ba, w_bb, w_o, g, b, *, layer, tm, alpha):
    m, d = x.shape
    row = lambda i: (i, 0)
    return pl.pallas_call(
        functools.partial(_post_kernel, alpha=alpha),
        grid=(m // tm,),
        in_specs=[pl.BlockSpec((tm, d), row), pl.BlockSpec((tm, oa.shape[1]), row),
                  pl.BlockSpec((tm, ob.shape[1]), row), _const_spec(wt_gate.shape)]
                 + [_layer_spec(a.shape, layer) for a in (w_ba, w_bb, w_o, g, b)],
        out_specs=pl.BlockSpec((tm, d), row),
        out_shape=jax.ShapeDtypeStruct((m, d), F32),
        compiler_params=_cparams(1),
        name="merge_outproj_ln",
    )(x, oa, ob, wt_gate, w_ba, w_bb, w_o, g, b)


def _ffn_kernel(x_ref, p_ref, wg_ref, wu_ref, wd_ref, wpg_ref, wpp_ref, g_ref, b_ref, o_ref,
                *, alpha):
    x = x_ref[...]
    xb = x.astype(BF16)
    hg = _dot(xb, wg_ref[...])
    hidden = (hg * _sigmoid(hg) * _dot(xb, wu_ref[...])).astype(BF16)
    y = alpha * x + _dot(hidden, wd_ref[...])
    ple = _sigmoid(_dot(xb, wpg_ref[...])) * _dot(p_ref[...].astype(BF16), wpp_ref[...])
    o_ref[...] = _layer_norm(y + ple, g_ref[...], b_ref[...])


def _ffn(x, p, w_g, w_u, w_d, w_pg, w_pp, g, b, *, layer, tm, alpha):
    m, d = x.shape
    row = lambda i: (i, 0)
    return pl.pallas_call(
        functools.partial(_ffn_kernel, alpha=alpha),
        grid=(m // tm,),
        in_specs=[pl.BlockSpec((tm, d), row),
                  pl.BlockSpec((None, tm, p.shape[2]), lambda i: (layer, i, 0))]
                 + [_layer_spec(a.shape, layer) for a in (w_g, w_u, w_d, w_pg, w_pp, g, b)],
        out_specs=pl.BlockSpec((tm, d), row),
        out_shape=jax.ShapeDtypeStruct((m, d), F32),
        compiler_params=_cparams(1),
        name="swiglu_ple_ln",
    )(x, p, w_g, w_u, w_d, w_pg, w_pp, g, b)


def _row_tile(m):
    for tm in (1024, 512, 256, 128):
        if m % tm == 0:
            return tm
    raise ValueError(f"row count {m} is not a multiple of 128")


def kernel(x_prompt, x_sample, p_prompt, p_sample, cache_fox_k, cache_fox_v, cache_fox_logf, cache_diff_k, cache_diff_v, w_in, b_forget, lambda_q1, lambda_k1, lambda_q2, lambda_k2, diff_norm_g, w_branch_fox, w_branch_diff, w_out, ln1_g, ln1_b, w_ffn_gate, w_ffn_up, w_ffn_down, w_ple_gate, w_ple_proj, ln2_g, ln2_b):
    batch, seq, d_model = x_prompt.shape
    dec_batch, dec_seq, _ = x_sample.shape
    depth = w_in.shape[0]
    past = cache_fox_k.shape[2]
    h_a = cache_fox_k.shape[3]
    h_b = cache_diff_k.shape[3]
    w_a = h_a * HD_A
    w_b = h_b * 2 * DH_B
    assert w_a == w_b and w_in.shape[2] == 3 * w_a + h_a + 3 * w_b + 2 * d_model
    assert HD_A == DH_B
    alpha = (2 * depth) ** 0.25
    m_p = batch * seq
    m_s = dec_batch * dec_seq
    tm_p = _row_tile(seq)
    tm_s = _row_tile(m_s)
    tq, tk = min(512, seq), min(256, seq)
    tk_dec = min(2048, past)
    assert seq % tq == 0 and tq % tk == 0 and tk % CHUNK == 0 and past % tk_dec == 0
    t_dec = past + dec_seq
    t_dec_pad = -(-t_dec // SCAN_BLOCK) * SCAN_BLOCK
    qk_scale = HD_A ** -0.5
    f_pad = 16

    xp = x_prompt.reshape(m_p, d_model)
    xs = x_sample.reshape(m_s, d_model)

    wt_in = jnp.swapaxes(w_in, 1, 2)
    kt_cache_a = jnp.transpose(cache_fox_k, (0, 1, 3, 4, 2)).reshape(depth, dec_batch, w_a, past)
    vt_cache_a = jnp.transpose(cache_fox_v, (0, 1, 3, 4, 2)).reshape(depth, dec_batch, w_a, past)
    kt_cache_b = jnp.transpose(cache_diff_k, (0, 1, 3, 4, 5, 2)).reshape(depth, dec_batch, w_b, past)
    lf_cache = jnp.swapaxes(cache_fox_logf, 2, 3).astype(F32)
    v_cache_b = cache_diff_v.reshape(depth, dec_batch, past * h_b, 2 * DH_B)

    post_w = (w_branch_fox.astype(BF16), w_branch_diff.astype(BF16), w_out.astype(BF16),
              ln1_g.reshape(depth, 1, d_model), ln1_b.reshape(depth, 1, d_model))
    ffn_w = (w_ffn_gate.astype(BF16), w_ffn_up.astype(BF16), w_ffn_down.astype(BF16),
             w_ple_gate.astype(BF16), w_ple_proj.astype(BF16),
             ln2_g.reshape(depth, 1, d_model), ln2_b.reshape(depth, 1, d_model))
    pp = p_prompt.reshape(depth, m_p, -1)
    ps = p_sample.reshape(depth, m_s, -1)

    o_f = 3 * w_a
    o_qb = o_f + h_a
    o_ga = o_qb + 3 * w_b

    stacked = (jnp.zeros((depth, batch, w_a, seq), F32), jnp.zeros((depth, batch, w_a, seq), F32),
               jnp.zeros((depth, batch, w_b, seq), F32),
               jnp.zeros((depth, batch, seq, h_b, 2 * DH_B), F32),
               jnp.zeros((depth, batch, h_a, seq), F32))
    new_s = [[] for _ in range(5)]
    for l in range(depth):
        lambda_init = 0.8 - 0.6 * math.exp(-0.3 * l)
        wt = wt_in[l]
        wt_qkv = jnp.concatenate([wt[:o_f], wt[o_qb:o_ga]], axis=0).astype(BF16)
        wt_f = jnp.pad(wt[o_f:o_qb], ((0, f_pad - h_a), (0, 0))).astype(BF16)
        b_f = jnp.pad(b_forget[l], (0, f_pad - h_a))
        wt_gate = wt[o_ga:].astype(BF16)

        (qat, ka, kat, vat, vat_b, qbt, kb, kbt, vb, vbt_b, logf_p) = _inproj_prompt(
            xp, wt_qkv, wt_f, b_f.reshape(f_pad, 1), stacked, layer=l, depth=depth, batch=batch,
            seq=seq, tm=tm_p, n_heads=h_a, q_scale=qk_scale * LOG2E)
        stacked = (kat, vat, kbt, vb, logf_p)
        (qa_s, qb_s, ka_s, va_s, kb_s, vb_s, ka_sb, va_sb, kb_sb, vb_sb, logf_s) = _inproj_decode(
            xs, wt_qkv, wt_f, b_f.reshape(1, f_pad), n_heads=h_a, q_scale=qk_scale)

        c_row_p, aug_p = _cumsum_aug(logf_p[l], LOG2E, width=w_a)
        lf_s = jnp.swapaxes(logf_s.reshape(dec_batch, dec_seq, h_a), 1, 2)
        lf_all = jnp.concatenate([lf_cache[l], lf_s], axis=2)
        lf_all = jnp.pad(lf_all, ((0, 0), (0, 0), (0, t_dec_pad - t_dec)))
        c_row_s = _cumsum_rows(lf_all.reshape(dec_batch * h_a, t_dec_pad), 1.0)
        c_row_s = c_row_s.reshape(dec_batch, h_a, t_dec_pad)
        c_col_s = jnp.swapaxes(c_row_s, 1, 2)

        lams = [a[l].reshape(1, DH_B) for a in (lambda_q1, lambda_k1, lambda_q2, lambda_k2)]
        g_diff = diff_norm_g[l].reshape(1, w_b)

        oa_p = _fox_prompt(qat, ka, vat_b, c_row_p, aug_p,
                           batch=batch, seq=seq, tq=tq, tk=tk, n_heads=h_a)
        ob_p = _diff_prompt(qbt, kb, vbt_b, lams, g_diff, batch=batch, seq=seq, tq=tq, tk=tk,
                            n_heads=h_b, lambda_init=lambda_init)
        oa_s = _fox_decode(qa_s, kt_cache_a, vt_cache_a, ka_sb, va_sb, c_col_s, c_row_s,
                           layer=l, tq=dec_seq, tk=tk_dec, n_heads=h_a)
        ob_s = _diff_decode(qb_s, kt_cache_b, v_cache_b, kb_sb, vb_sb, lams, g_diff,
                            layer=l, tq=dec_seq, tk=tk_dec, n_heads=h_b, lambda_init=lambda_init)

        xp = _post(xp, oa_p, ob_p, wt_gate, *post_w, layer=l, tm=tm_p, alpha=alpha)
        xp = _ffn(xp, pp, *ffn_w, layer=l, tm=tm_p, alpha=alpha)
        xs = _post(xs, oa_s, ob_s, wt_gate, *post_w, layer=l, tm=tm_s, alpha=alpha)
        xs = _ffn(xs, ps, *ffn_w, layer=l, tm=tm_s, alpha=alpha)

        for lst, r in zip(new_s, (ka_s, va_s, logf_s, kb_s, vb_s)):
            lst.append(r)

    kat, vat, kbt, vb, logf_p = stacked
    ka_s, va_s, logf_s, kb_s, vb_s = [jnp.stack(a) for a in new_s]
    fox_k_p = jnp.transpose(kat.reshape(depth, batch, h_a, HD_A, seq), (0, 1, 4, 2, 3))
    fox_v_p = jnp.transpose(vat.reshape(depth, batch, h_a, HD_A, seq), (0, 1, 4, 2, 3))
    fox_lf_p = jnp.swapaxes(logf_p, 2, 3)
    diff_k_p = jnp.transpose(kbt.reshape(depth, batch, h_b, 2, DH_B, seq), (0, 1, 5, 2, 3, 4))
    return (xp.reshape(batch, seq, d_model), xs.reshape(dec_batch, dec_seq, d_model),
            fox_k_p, fox_v_p, fox_lf_p, diff_k_p, vb,
            ka_s.reshape(depth, dec_batch, dec_seq, h_a, HD_A),
            va_s.reshape(depth, dec_batch, dec_seq, h_a, HD_A),
            logf_s.reshape(depth, dec_batch, dec_seq, h_a),
            kb_s.reshape(depth, dec_batch, dec_seq, h_b, 2, DH_B),
            vb_s.reshape(depth, dec_batch, dec_seq, h_b, 2 * DH_B))
```

```python
import functools
import math

import jax
import jax.numpy as jnp
import numpy as np
from jax import lax
from jax.experimental import pallas as pl
from jax.experimental.pallas import tpu as pltpu

F32 = jnp.float32
BF16 = jnp.bfloat16

HD_A = 64
DH_B = 64
CHUNK = 64
CHUNK_SHIFT = 6
LN_EPS = 1e-5
RMS_EPS = 1e-5
NEG_INF = -1e30
LOG2E = math.log2(math.e)
LANES = 128
VMEM_LIMIT = 56 * 1024 * 1024


def _cparams(n_axes):
    return pltpu.CompilerParams(dimension_semantics=("arbitrary",) * n_axes,
                                vmem_limit_bytes=VMEM_LIMIT)


def _const_spec(shape):
    zeros = (0,) * len(shape)
    return pl.BlockSpec(shape, lambda *_: zeros, pipeline_mode=pl.Buffered(1))


def _layer_spec(shape, layer):
    zeros = (0,) * (len(shape) - 1)
    return pl.BlockSpec((None,) + tuple(shape[1:]), lambda *_: (layer,) + zeros,
                        pipeline_mode=pl.Buffered(1))


def _dot(a, b):
    return jnp.dot(a, b, preferred_element_type=F32)


def _dot_nt(a, b):
    return lax.dot_general(a, b, (((1,), (1,)), ((), ())), preferred_element_type=F32)


def _sigmoid(x):
    return 1.0 / (1.0 + jnp.exp(-x))


def _log_sigmoid(z):
    return jnp.minimum(z, 0.0) - jnp.log1p(jnp.exp(-jnp.abs(z)))


def _layer_norm(y, g, b):
    mu = jnp.mean(y, axis=-1, keepdims=True)
    yc = y - mu
    var = jnp.mean(yc * yc, axis=-1, keepdims=True)
    return yc * lax.rsqrt(var + LN_EPS) * g + b


def _inproj_prompt_kernel(x_ref, wt_ref, wft_ref, bf_ref, *refs, width, n_heads, q_scale):
    (qat_ref, ka_ref, kat_ref, vat_ref, vatb_ref,
     qbt_ref, kb_ref, kbt_ref, vb_ref, vbtb_ref, logf_ref) = refs[-11:]
    xb = x_ref[...].astype(BF16)

    def proj_t(i):
        return _dot_nt(wt_ref[i * width:(i + 1) * width, :], xb)

    qat_ref[0] = (proj_t(0) * q_scale).astype(BF16)
    kat = proj_t(1)
    kat_ref[0, 0] = kat
    ka_ref[...] = kat.T.astype(BF16)
    vat = proj_t(2)
    vat_ref[0, 0] = vat
    vatb_ref[0] = vat.astype(BF16)
    qbt_ref[0] = (proj_t(3) * q_scale).astype(BF16)
    kbt = proj_t(4)
    kbt_ref[0, 0] = kbt
    kb_ref[...] = kbt.T.astype(BF16)
    vbt = proj_t(5)
    vbtb_ref[0] = vbt.astype(BF16)
    vb = vbt.T
    for h in range(width // LANES):
        vb_ref[0, 0, :, h, :] = vb[:, h * LANES:(h + 1) * LANES]
    z = _dot_nt(wft_ref[...], xb) + bf_ref[...]
    logf_ref[0, 0] = _log_sigmoid(z)[:n_heads, :]


def _inproj_prompt(x, wt_qkv, wt_f, b_f_col, stacked, *, layer, depth, batch, seq, tm, n_heads,
                   q_scale):
    m, d = x.shape
    width = wt_qkv.shape[0] // 6
    nt = seq // tm
    row = lambda i: (i, 0)
    tr = lambda i: (i // nt, 0, i % nt)
    tr_l = lambda i: (layer, i // nt, 0, i % nt)
    t_blk = pl.BlockSpec((1, width, tm), tr)
    t_blk_l = pl.BlockSpec((1, 1, width, tm), tr_l)
    n_blk = pl.BlockSpec((tm, width), row)
    t_f32 = jax.ShapeDtypeStruct((depth, batch, width, seq), F32)
    t_b16 = jax.ShapeDtypeStruct((batch, width, seq), BF16)
    n_b16 = jax.ShapeDtypeStruct((m, width), BF16)
    vb_heads = width // LANES
    out_specs = [t_blk, n_blk, t_blk_l, t_blk_l, t_blk, t_blk, n_blk, t_blk_l,
                 pl.BlockSpec((1, 1, tm, vb_heads, LANES),
                              lambda i: (layer, i // nt, i % nt, 0, 0)),
                 t_blk, pl.BlockSpec((1, 1, n_heads, tm), tr_l)]
    out_shape = [t_b16, n_b16, t_f32, t_f32, t_b16, t_b16, n_b16, t_f32,
                 jax.ShapeDtypeStruct((depth, batch, seq, vb_heads, LANES), F32),
                 t_b16, jax.ShapeDtypeStruct((depth, batch, n_heads, seq), F32)]
    stacked_outputs = (2, 3, 7, 8, 10)
    in_specs = [pl.BlockSpec((tm, d), row), _const_spec(wt_qkv.shape), _const_spec(wt_f.shape),
                _const_spec(b_f_col.shape)]
    args = [x, wt_qkv, wt_f, b_f_col]
    aliases = {len(args) + i: o for i, o in enumerate(stacked_outputs)}
    in_specs += [pl.BlockSpec(memory_space=pl.ANY)] * len(stacked)
    args += list(stacked)
    return pl.pallas_call(
        functools.partial(_inproj_prompt_kernel, width=width, n_heads=n_heads, q_scale=q_scale),
        grid=(m // tm,),
        in_specs=in_specs,
        out_specs=out_specs,
        out_shape=out_shape,
        input_output_aliases=aliases,
        compiler_params=_cparams(1),
        name="inproj_prompt",
    )(*args)


def _inproj_decode_kernel(x_ref, wt_ref, wft_ref, bf_ref,
                          qa_ref, qb_ref, ka_ref, va_ref, kb_ref, vb_ref,
                          kab_ref, vab_ref, kbb_ref, vbb_ref, logf_ref, *, width, n_heads, q_scale):
    xb = x_ref[...].astype(BF16)

    def proj(i):
        return _dot_nt(xb, wt_ref[i * width:(i + 1) * width, :])

    qa_ref[...] = (proj(0) * q_scale).astype(BF16)
    qb_ref[...] = (proj(3) * q_scale).astype(BF16)
    for i, (full_ref, half_ref) in ((1, (ka_ref, kab_ref)), (2, (va_ref, vab_ref)),
                                    (4, (kb_ref, kbb_ref)), (5, (vb_ref, vbb_ref))):
        val = proj(i)
        full_ref[...] = val
        half_ref[...] = val.astype(BF16)
    z = _dot_nt(xb, wft_ref[...]) + bf_ref[...]
    logf_ref[...] = _log_sigmoid(z)[:, :n_heads]


def _inproj_decode(x, wt_qkv, wt_f, b_f_row, *, n_heads, q_scale):
    m, d = x.shape
    width = wt_qkv.shape[0] // 6
    out_shape = ([jax.ShapeDtypeStruct((m, width), BF16)] * 2
                 + [jax.ShapeDtypeStruct((m, width), F32)] * 4
                 + [jax.ShapeDtypeStruct((m, width), BF16)] * 4
                 + [jax.ShapeDtypeStruct((m, n_heads), F32)])
    return pl.pallas_call(
        functools.partial(_inproj_decode_kernel, width=width, n_heads=n_heads, q_scale=q_scale),
        out_shape=out_shape,
        compiler_params=pltpu.CompilerParams(vmem_limit_bytes=VMEM_LIMIT),
        name="inproj_decode",
    )(x, wt_qkv, wt_f, b_f_row)


SCAN_BLOCK = 256


def _split3(a):
    a1 = a.astype(BF16)
    r1 = a - a1.astype(F32)
    a2 = r1.astype(BF16)
    a3 = (r1 - a2.astype(F32)).astype(BF16)
    return a1, a2, a3


def _split3_const(x):
    pieces, rest = [], np.float32(x)
    for _ in range(3):
        p = np.float32(np.asarray(rest, dtype=jnp.bfloat16))
        pieces.append(float(p))
        rest = np.float32(rest - p)
    return pieces


def _cumsum_kernel(lf_ref, c_ref, *, n_blk, scale):
    rows = lf_ref.shape[0]
    r = lax.broadcasted_iota(jnp.int32, (SCAN_BLOCK, SCAN_BLOCK), 0)
    c = lax.broadcasted_iota(jnp.int32, (SCAN_BLOCK, SCAN_BLOCK), 1)
    upper = jnp.where(r <= c, 1.0, 0.0).astype(BF16)
    carry = jnp.zeros((rows, 1), F32)
    for j in range(n_blk):
        sl = slice(j * SCAN_BLOCK, (j + 1) * SCAN_BLOCK)
        a1, a2, a3 = _split3(lf_ref[:, sl])
        blk = _dot(a1, upper) + _dot(a2, upper) + _dot(a3, upper) + carry
        c_ref[:, sl] = blk * scale
        carry = blk[:, SCAN_BLOCK - 1:SCAN_BLOCK]


def _cumsum_rows(lf_rows, scale):
    rows, t = lf_rows.shape
    assert t % SCAN_BLOCK == 0
    return pl.pallas_call(
        functools.partial(_cumsum_kernel, n_blk=t // SCAN_BLOCK, scale=scale),
        out_shape=jax.ShapeDtypeStruct((rows, t), F32),
        compiler_params=pltpu.CompilerParams(vmem_limit_bytes=VMEM_LIMIT),
        name="logf_cumsum",
    )(lf_rows)


AUG_PIECES = 3


def _aug_lane(h, piece):
    return AUG_PIECES * h + piece


def _cumsum_aug_kernel(lf_rows_ref, lf_cols_ref, place_ref, c_ref, aug_ref, *, n_blk, scale):
    rows = lf_rows_ref.shape[0]
    batch, _, width = aug_ref.shape
    r = lax.broadcasted_iota(jnp.int32, (SCAN_BLOCK, SCAN_BLOCK), 0)
    c = lax.broadcasted_iota(jnp.int32, (SCAN_BLOCK, SCAN_BLOCK), 1)
    upper = jnp.where(r <= c, 1.0, 0.0).astype(BF16)
    lower = jnp.where(c <= r, 1.0, 0.0).astype(BF16)
    carry_r = jnp.zeros((rows, 1), F32)
    carry_c = jnp.zeros((1, rows), F32)
    for j in range(n_blk):
        sl = slice(j * SCAN_BLOCK, (j + 1) * SCAN_BLOCK)
        a1, a2, a3 = _split3(lf_rows_ref[:, sl])
        blk_r = _dot(a1, upper) + _dot(a2, upper) + _dot(a3, upper) + carry_r
        c_ref[:, sl] = blk_r * scale
        carry_r = blk_r[:, SCAN_BLOCK - 1:SCAN_BLOCK]
        b1, b2, b3 = _split3(lf_cols_ref[sl, :])
        blk_c = _dot(lower, b1) + _dot(lower, b2) + _dot(lower, b3) + carry_c
        carry_c = blk_c[SCAN_BLOCK - 1:SCAN_BLOCK, :]
        pieces = jnp.concatenate(_split3(blk_c * scale), axis=1)
        placed = _dot(pieces, place_ref[...]).astype(BF16)
        for b in range(batch):
            aug_ref[b, sl, :] = placed[:, b * width:(b + 1) * width]


def _placement_matrix(batch, n_heads, width):
    rows = batch * n_heads
    place = np.zeros((AUG_PIECES * rows, batch * width), np.float32)
    for b in range(batch):
        for h in range(n_heads):
            for p in range(AUG_PIECES):
                place[p * rows + b * n_heads + h, b * width + _aug_lane(h, p)] = 1.0
    return jnp.asarray(place, BF16)


def _cumsum_aug(lf, scale):
    batch, n_heads, t = lf.shape
    assert t % SCAN_BLOCK == 0 and AUG_PIECES * n_heads <= LANES
    width = LANES
    rows = batch * n_heads
    c_rows, aug = pl.pallas_call(
        functools.partial(_cumsum_aug_kernel, n_blk=t // SCAN_BLOCK, scale=scale),
        out_shape=[jax.ShapeDtypeStruct((rows, t), F32),
                   jax.ShapeDtypeStruct((batch, t, width), BF16)],
        compiler_params=pltpu.CompilerParams(vmem_limit_bytes=VMEM_LIMIT),
        name="logf_cumsum_aug",
    )(lf.reshape(rows, t), jnp.transpose(lf, (2, 0, 1)).reshape(t, rows),
      _placement_matrix(batch, n_heads, width))
    return c_rows.reshape(batch, n_heads, t), aug


def _lambda_scalar(lq1, lk1, lq2, lk2, lambda_init):
    return (jnp.exp(jnp.sum(lq1 * lk1, axis=1, keepdims=True))
            - jnp.exp(jnp.sum(lq2 * lk2, axis=1, keepdims=True)) + lambda_init)


def _head_rms(o, g_row, lambda_init):
    o = o * lax.rsqrt(jnp.mean(o * o, axis=-1, keepdims=True) + RMS_EPS)
    return o * g_row * (1.0 - lambda_init)


def _alibi_slope(h, n_heads):
    return 2.0 ** (-8.0 * (h + 1) / n_heads)


def _sublane_half_masks(cols):
    row = lax.broadcasted_iota(jnp.int32, (LANES, cols), 0)
    lo = jnp.where(row < HD_A, 1.0, 0.0)
    return lo, 1.0 - lo


SUM_ROWS = 16


def _with_sum_rows(v_t):
    return jnp.concatenate([v_t, jnp.ones((SUM_ROWS, v_t.shape[1]), BF16)], axis=0)


def _flash_step_refs(i, t, shift, v_aug, m_ref, acc_ref, qs=slice(None)):
    m = m_ref[i, :, qs]
    m_new = jnp.maximum(m, jnp.max(t, axis=0, keepdims=True) + shift)
    p = jnp.exp2(t - (m_new - shift)).astype(BF16)
    acc_ref[i, :, qs] = jnp.exp2(m - m_new) * acc_ref[i, :, qs] + _dot(v_aug, p)
    m_ref[i, :, qs] = m_new


def _flash_init_refs(m_ref, acc_ref):
    m_ref[...] = jnp.full(m_ref.shape, NEG_INF, F32)
    acc_ref[...] = jnp.zeros(acc_ref.shape, F32)


def _flash_result(i, d, acc_ref):
    acc = acc_ref[i]
    return acc[:d] / acc[d:d + 1]


def _stream_scratch(n_streams, d, tq):
    return [pltpu.VMEM((n_streams, 2 * LANES, tq), BF16), pltpu.VMEM((n_streams, 1, tq), F32),
            pltpu.VMEM((n_streams, d + SUM_ROWS, tq), F32)]


def _fox_prompt_kernel(qt_ref, k_ref, vt_ref, crow_ref, aug_ref, o_ref, qm_ref, m_ref, acc_ref,
                       *, tq, tk, n_heads):
    qi = pl.program_id(1)
    n_diag = tq // tk
    lo, hi = _sublane_half_masks(tq)
    row = lax.broadcasted_iota(jnp.int32, (LANES, tq), 0)
    for h in range(n_heads):
        blk = slice((h // 2) * LANES, (h // 2 + 1) * LANES)
        qm_ref[h, :LANES] = (qt_ref[0, blk, :].astype(F32) * (lo, hi)[h % 2]).astype(BF16)
        picks = (row >= _aug_lane(h, 0)) & (row < _aug_lane(h, AUG_PIECES))
        qm_ref[h, LANES:] = jnp.where(picks, -1.0, 0.0).astype(BF16)
    _flash_init_refs(m_ref, acc_ref)

    def tile(ks, mask, qs=slice(None)):
        v_all = vt_ref[0, :, pl.ds(ks, tk)]
        scores = []
        for h in range(n_heads):
            blk = slice((h // 2) * LANES, (h // 2 + 1) * LANES)
            k_aug = jnp.concatenate([k_ref[pl.ds(ks, tk), blk], aug_ref[0, pl.ds(ks, tk), :]],
                                    axis=1)
            t = _dot(k_aug, qm_ref[h, :, qs])
            if mask is not None:
                t = jnp.where(mask, t, NEG_INF)
            scores.append(t)
        for h in range(n_heads):
            _flash_step_refs(h, scores[h], crow_ref[0, h:h + 1, qs],
                             _with_sum_rows(v_all[h * HD_A:(h + 1) * HD_A]), m_ref, acc_ref, qs)

    def full_tile(j, carry):
        tile(pl.multiple_of(j * tk, tk), None)
        return carry

    lax.fori_loop(0, qi * n_diag, full_tile, 0)
    for d in range(n_diag):
        nq = tq - d * tk
        r = lax.broadcasted_iota(jnp.int32, (tk, nq), 0)
        c = lax.broadcasted_iota(jnp.int32, (tk, nq), 1)
        tile(pl.multiple_of(qi * tq + d * tk, tk), r <= c, slice(d * tk, tq))
    for pair in range(n_heads // 2):
        blk = slice(pair * LANES, (pair + 1) * LANES)
        o_t = jnp.concatenate([_flash_result(2 * pair, HD_A, acc_ref),
                               _flash_result(2 * pair + 1, HD_A, acc_ref)], axis=0)
        o_ref[:, blk] = o_t.T.astype(BF16)


def _fox_prompt(qt, k, vt, c_row, aug, *, batch, seq, tq, tk, n_heads):
    width = n_heads * HD_A
    nq = seq // tq
    return pl.pallas_call(
        functools.partial(_fox_prompt_kernel, tq=tq, tk=tk, n_heads=n_heads),
        grid=(batch, nq),
        in_specs=[pl.BlockSpec((1, width, tq), lambda b, i: (b, 0, i)),
                  pl.BlockSpec((seq, width), lambda b, i: (b, 0)),
                  pl.BlockSpec((1, width, seq), lambda b, i: (b, 0, 0)),
                  pl.BlockSpec((1, n_heads, tq), lambda b, i: (b, 0, i)),
                  pl.BlockSpec((1, seq, LANES), lambda b, i: (b, 0, 0))],
        out_specs=pl.BlockSpec((tq, width), lambda b, i: (b * nq + i, 0)),
        out_shape=jax.ShapeDtypeStruct((batch * seq, width), BF16),
        scratch_shapes=_stream_scratch(n_heads, HD_A, tq),
        compiler_params=_cparams(2),
        name="fox_prompt",
    )(qt, k, vt, c_row, aug)


def _diff_prompt_kernel(qt_ref, k_ref, vt_ref, lq1_ref, lk1_ref, lq2_ref, lk2_ref, g_ref, o_ref,
                        qm_ref, m_ref, acc_ref, *, tq, tk, n_heads, lambda_init):
    qi = pl.program_id(1)
    n_diag = tq // tk
    lo, hi = _sublane_half_masks(tq)
    row = lax.broadcasted_iota(jnp.int32, (LANES, tq), 0)
    for i in range(2 * n_heads):
        blk = slice((i // 2) * LANES, (i // 2 + 1) * LANES)
        qm_ref[i, :LANES] = (qt_ref[0, blk, :].astype(F32) * (lo, hi)[i % 2]).astype(BF16)
        s1, s2, s3 = _split3_const(_alibi_slope(i // 2, n_heads) * LOG2E)
        slope_rows = jnp.where(row == 0, s1, jnp.where(row == 1, s2, jnp.where(row == 2, s3, 0.0)))
        qm_ref[i, LANES:] = slope_rows.astype(BF16)
    _flash_init_refs(m_ref, acc_ref)
    assert tk <= 256
    lane = lax.broadcasted_iota(jnp.int32, (tk, LANES), 1)
    key_off = lax.broadcasted_iota(jnp.int32, (tk, LANES), 0).astype(F32)
    key_off = jnp.where(lane < AUG_PIECES, key_off, 0.0).astype(BF16)
    q_off = lax.broadcasted_iota(jnp.int32, (1, tq), 1).astype(F32)

    def tile(ks, scores_of, shift_of, qs=slice(None)):
        k_all = k_ref[pl.ds(ks, tk), :]
        v_all = vt_ref[0, :, pl.ds(ks, tk)]
        scores = [scores_of(k_all[:, (i // 2) * LANES:(i // 2 + 1) * LANES], i)
                  for i in range(2 * n_heads)]
        for i in range(2 * n_heads):
            h = i // 2
            _flash_step_refs(i, scores[i], shift_of(_alibi_slope(h, n_heads) * LOG2E),
                             _with_sum_rows(v_all[h * LANES:(h + 1) * LANES]), m_ref, acc_ref, qs)

    def full_tile(j, carry):
        off = (j * tk - qi * tq).astype(F32) - q_off
        tile(pl.multiple_of(j * tk, tk),
             lambda k2, i: _dot(jnp.concatenate([k2, key_off], axis=1), qm_ref[i]),
             lambda s2: s2 * off)
        return carry

    lax.fori_loop(0, qi * n_diag, full_tile, 0)
    for d in range(n_diag):
        qs = slice(d * tk, tq)
        r = lax.broadcasted_iota(jnp.int32, (tk, tq - d * tk), 0)
        c = lax.broadcasted_iota(jnp.int32, (tk, tq - d * tk), 1)
        dist = jnp.abs(r - c).astype(F32)
        visible = (lax.shift_right_logical(r, CHUNK_SHIFT)
                   <= lax.shift_right_logical(c, CHUNK_SHIFT))

        def diag_scores(k2, i):
            slope2 = _alibi_slope(i // 2, n_heads) * LOG2E
            return jnp.where(visible, _dot(k2, qm_ref[i, :LANES, qs]) - slope2 * dist, NEG_INF)

        tile(pl.multiple_of(qi * tq + d * tk, tk), diag_scores, lambda s2: 0.0, qs)
    lam = _lambda_scalar(lq1_ref[...], lk1_ref[...], lq2_ref[...], lk2_ref[...], lambda_init)
    for h in range(n_heads):
        blk = slice(h * LANES, (h + 1) * LANES)
        o = (_flash_result(2 * h, LANES, acc_ref) - lam * _flash_result(2 * h + 1, LANES, acc_ref)).T
        o_ref[:, blk] = _head_rms(o, g_ref[:, blk], lambda_init).astype(BF16)


def _diff_prompt(qt, k, vt, lams, g, *, batch, seq, tq, tk, n_heads, lambda_init):
    width = n_heads * 2 * DH_B
    nq = seq // tq
    return pl.pallas_call(
        functools.partial(_diff_prompt_kernel, tq=tq, tk=tk, n_heads=n_heads,
                          lambda_init=lambda_init),
        grid=(batch, nq),
        in_specs=[pl.BlockSpec((1, width, tq), lambda b, i: (b, 0, i)),
                  pl.BlockSpec((seq, width), lambda b, i: (b, 0)),
                  pl.BlockSpec((1, width, seq), lambda b, i: (b, 0, 0))]
                 + [_const_spec(a.shape) for a in lams] + [_const_spec(g.shape)],
        out_specs=pl.BlockSpec((tq, width), lambda b, i: (b * nq + i, 0)),
        out_shape=jax.ShapeDtypeStruct((batch * seq, width), BF16),
        scratch_shapes=_stream_scratch(2 * n_heads, LANES, tq),
        compiler_params=_cparams(2),
        name="diff_prompt",
    )(qt, k, vt, *lams, g)


def _lane_half_masks(rows):
    lane = lax.broadcasted_iota(jnp.int32, (rows, LANES), 1)
    lo = jnp.where(lane < HD_A, 1.0, 0.0)
    return lo, 1.0 - lo


def _masked_q(q2, mask):
    return (q2.astype(F32) * mask).astype(BF16)


def _flash_init(m_ref, l_ref, acc_ref):
    m_ref[...] = jnp.full(m_ref.shape, NEG_INF, F32)
    l_ref[...] = jnp.zeros(l_ref.shape, F32)
    acc_ref[...] = jnp.zeros(acc_ref.shape, F32)


def _flash_step(s, pv, m_ref, l_ref, acc_ref):
    m_prev = m_ref[...]
    m_new = jnp.maximum(m_prev, jnp.max(s, axis=1, keepdims=True))
    alpha = jnp.exp(m_prev - m_new)
    p = jnp.exp(s - m_new)
    l_ref[...] = alpha * l_ref[...] + jnp.sum(p, axis=1, keepdims=True)
    acc_ref[...] = alpha * acc_ref[...] + pv(p.astype(BF16))
    m_ref[...] = m_new


def _fox_decode_kernel(q_ref, kc_ref, vc_ref, kn_ref, vn_ref, ccol_ref, crow_ref, o_ref,
                       m_ref, l_ref, acc_ref, *, tq, tk, n_kv, n_heads):
    j = pl.program_id(1)
    lo, hi = _lane_half_masks(tq)

    @pl.when(j == 0)
    def _():
        _flash_init(m_ref, l_ref, acc_ref)

    def head_loop(scores_of, pv_of, bias_of, mask):
        pending = []
        for pair in range(n_heads // 2):
            blk = slice(pair * LANES, (pair + 1) * LANES)
            q2 = q_ref[:, blk]
            scores = scores_of(blk)
            pv = pv_of(blk)
            for half, hm in enumerate((lo, hi)):
                s = scores(_masked_q(q2, hm)) + bias_of(2 * pair + half)
                if mask is not None:
                    s = jnp.where(mask, s, NEG_INF)
                pending.append((s, pv))
        for h, (s, pv) in enumerate(pending):
            _flash_step(s, pv, m_ref.at[h], l_ref.at[h], acc_ref.at[h])

    def cache_tile():
        ks = pl.multiple_of(j * tk, tk)

        def scores_of(blk):
            kt = kc_ref[0, 0, blk, :].astype(BF16)
            return lambda qm: _dot(qm, kt)

        def pv_of(blk):
            vt = vc_ref[0, 0, blk, :].astype(BF16)
            return lambda p: _dot_nt(p, vt)

        head_loop(scores_of, pv_of,
                  lambda h: ccol_ref[0, :, h:h + 1] - crow_ref[0, h:h + 1, pl.ds(ks, tk)], None)

    cache_tile()

    @pl.when(j == n_kv - 1)
    def _():
        past = n_kv * tk
        q_pos = lax.broadcasted_iota(jnp.int32, (tq, tq), 0)
        k_pos = lax.broadcasted_iota(jnp.int32, (tq, tq), 1)
        head_loop(lambda blk: (lambda qm: _dot_nt(qm, kn_ref[:, blk])),
                  lambda blk: (lambda p: _dot(p, vn_ref[:, blk])),
                  lambda h: ccol_ref[0, :, h:h + 1] - crow_ref[0, h:h + 1, past:past + tq],
                  k_pos <= q_pos)
        for pair in range(n_heads // 2):
            blk = slice(pair * LANES, (pair + 1) * LANES)
            o0 = acc_ref[2 * pair] / l_ref[2 * pair]
            o1 = acc_ref[2 * pair + 1] / l_ref[2 * pair + 1]
            o_ref[:, blk] = (o0 * lo + o1 * hi).astype(BF16)


def _fox_decode(q, kt_cache, vt_cache, k_new, v_new, c_col, c_row, *, layer, tq, tk, n_heads):
    _, batch, width, past = kt_cache.shape
    n_kv = past // tk
    new_blk = pl.BlockSpec((tq, width), lambda b, j: (b, 0))
    cache_blk = pl.BlockSpec((1, 1, width, tk),
                             lambda b, j: (layer, b, 0, j))
    return pl.pallas_call(
        functools.partial(_fox_decode_kernel, tq=tq, tk=tk, n_kv=n_kv, n_heads=n_heads),
        grid=(batch, n_kv),
        in_specs=[new_blk, cache_blk, cache_blk, new_blk, new_blk,
                  pl.BlockSpec((1, tq, n_heads), lambda b, j: (b, past // tq, 0)),
                  pl.BlockSpec((1, n_heads, c_row.shape[2]), lambda b, j: (b, 0, 0))],
        out_specs=new_blk,
        out_shape=jax.ShapeDtypeStruct((batch * tq, width), BF16),
        scratch_shapes=[pltpu.VMEM((n_heads, tq, 1), F32), pltpu.VMEM((n_heads, tq, 1), F32),
                        pltpu.VMEM((n_heads, tq, LANES), F32)],
        compiler_params=_cparams(2),
        name="fox_decode",
    )(q, kt_cache, vt_cache, k_new, v_new, c_col, c_row)


def _diff_decode_kernel(q_ref, kc_ref, vc_ref, kn_ref, vn_ref, lq1_ref, lk1_ref, lq2_ref, lk2_ref,
                        g_ref, o_ref, m_ref, l_ref, acc_ref,
                        *, tq, tk, n_kv, n_heads, lambda_init):
    j = pl.program_id(1)
    lo, hi = _lane_half_masks(tq)
    past = n_kv * tk
    q_abs = (past + lax.broadcasted_iota(jnp.int32, (tq, 1), 0)).astype(F32)

    @pl.when(j == 0)
    def _():
        _flash_init(m_ref, l_ref, acc_ref)

    def head_loop(scores_of, pv_of, dist):
        pending = []
        for h in range(n_heads):
            blk = slice(h * LANES, (h + 1) * LANES)
            slope = _alibi_slope(h, n_heads)
            q2 = q_ref[:, blk]
            scores = scores_of(blk)
            pv = pv_of(h)
            for hm in (lo, hi):
                pending.append((scores(_masked_q(q2, hm)) - slope * dist, pv))
        for idx, (s, pv) in enumerate(pending):
            _flash_step(s, pv, m_ref.at[idx], l_ref.at[idx], acc_ref.at[idx])

    def cache_tile():
        k_abs = (j * tk + lax.broadcasted_iota(jnp.int32, (1, tk), 1)).astype(F32)

        def scores_of(blk):
            kt = kc_ref[0, 0, blk, :].astype(BF16)
            return lambda qm: _dot(qm, kt)

        def pv_of(h):
            v2 = vc_ref[0, 0, pl.ds(h, tk, stride=n_heads), :].astype(BF16)
            return lambda p: _dot(p, v2)

        head_loop(scores_of, pv_of, jnp.abs(q_abs - k_abs))

    cache_tile()

    @pl.when(j == n_kv - 1)
    def _():
        assert past % CHUNK == 0 and tq <= CHUNK
        k_abs = (past + lax.broadcasted_iota(jnp.int32, (1, tq), 1)).astype(F32)
        head_loop(lambda blk: (lambda qm: _dot_nt(qm, kn_ref[:, blk])),
                  lambda h: (lambda p: _dot(p, vn_ref[:, h * LANES:(h + 1) * LANES])),
                  jnp.abs(q_abs - k_abs))
        lam = _lambda_scalar(lq1_ref[...], lk1_ref[...], lq2_ref[...], lk2_ref[...], lambda_init)
        for h in range(n_heads):
            blk = slice(h * LANES, (h + 1) * LANES)
            o = acc_ref[2 * h] / l_ref[2 * h] - lam * (acc_ref[2 * h + 1] / l_ref[2 * h + 1])
            o_ref[:, blk] = _head_rms(o, g_ref[:, blk], lambda_init).astype(BF16)


def _diff_decode(q, kt_cache, v_cache, k_new, v_new, lams, g, *, layer, tq, tk, n_heads,
                 lambda_init):
    _, batch, width, past = kt_cache.shape
    n_kv = past // tk
    new_blk = pl.BlockSpec((tq, width), lambda b, j: (b, 0))
    return pl.pallas_call(
        functools.partial(_diff_decode_kernel, tq=tq, tk=tk, n_kv=n_kv, n_heads=n_heads,
                          lambda_init=lambda_init),
        grid=(batch, n_kv),
        in_specs=[new_blk,
                  pl.BlockSpec((1, 1, width, tk),
                               lambda b, j: (layer, b, 0, j)),
                  pl.BlockSpec((1, 1, tk * n_heads, LANES),
                               lambda b, j: (layer, b, j, 0)),
                  new_blk, new_blk]
                 + [_const_spec(a.shape) for a in lams] + [_const_spec(g.shape)],
        out_specs=new_blk,
        out_shape=jax.ShapeDtypeStruct((batch * tq, width), BF16),
        scratch_shapes=[pltpu.VMEM((2 * n_heads, tq, 1), F32),
                        pltpu.VMEM((2 * n_heads, tq, 1), F32),
                        pltpu.VMEM((2 * n_heads, tq, LANES), F32)],
        compiler_params=_cparams(2),
        name="diff_decode",
    )(q, kt_cache, v_cache, k_new, v_new, *lams, g)


def _post_kernel(x_ref, oa_ref, ob_ref, wgt_ref, wba_ref, wbb_ref, wo_ref, g_ref, b_ref, o_ref,
                 *, alpha):
    x = x_ref[...]
    xb = x.astype(BF16)
    d = x.shape[1]
    merged = _sigmoid(_dot_nt(xb, wgt_ref[:d, :])) * _dot(oa_ref[...], wba_ref[...])
    merged = merged + _sigmoid(_dot_nt(xb, wgt_ref[d:, :])) * _dot(ob_ref[...], wbb_ref[...])
    y = alpha * x + _dot(merged.astype(BF16), wo_ref[...])
    o_ref[...] = _layer_norm(y, g_ref[...], b_ref[...])


def _post(x, oa, ob, wt_gate, w_ba, w_bb, w_o, g, b, *, layer, tm, alpha):
    m, d = x.shape
    row = lambda i: (i, 0)
    return pl.pallas_call(
        functools.partial(_post_kernel, alpha=alpha),
        grid=(m // tm,),
        in_specs=[pl.BlockSpec((tm, d), row), pl.BlockSpec((tm, oa.shape[1]), row),
                  pl.BlockSpec((tm, ob.shape[1]), row), _const_spec(wt_gate.shape)]
                 + [_layer_spec(a.shape, layer) for a in (w_ba, w_bb, w_o, g, b)],
        out_specs=pl.BlockSpec((tm, d), row),
        out_shape=jax.ShapeDtypeStruct((m, d), F32),
        compiler_params=_cparams(1),
        name="merge_outproj_ln",
    )(x, oa, ob, wt_gate, w_ba, w_bb, w_o, g, b)


def _ffn_kernel(x_ref, p_ref, wg_ref, wu_ref, wd_ref, wpg_ref, wpp_ref, g_ref, b_ref, o_ref,
                *, alpha):
    x = x_ref[...]
    xb = x.astype(BF16)
    hg = _dot(xb, wg_ref[...])
    hidden = (hg * _sigmoid(hg) * _dot(xb, wu_ref[...])).astype(BF16)
    y = alpha * x + _dot(hidden, wd_ref[...])
    ple = _sigmoid(_dot(xb, wpg_ref[...])) * _dot(p_ref[...].astype(BF16), wpp_ref[...])
    o_ref[...] = _layer_norm(y + ple, g_ref[...], b_ref[...])


def _ffn(x, p, w_g, w_u, w_d, w_pg, w_pp, g, b, *, layer, tm, alpha):
    m, d = x.shape
    row = lambda i: (i, 0)
    return pl.pallas_call(
        functools.partial(_ffn_kernel, alpha=alpha),
        grid=(m // tm,),
        in_specs=[pl.BlockSpec((tm, d), row),
                  pl.BlockSpec((None, tm, p.shape[2]), lambda i: (layer, i, 0))]
                 + [_layer_spec(a.shape, layer) for a in (w_g, w_u, w_d, w_pg, w_pp, g, b)],
        out_specs=pl.BlockSpec((tm, d), row),
        out_shape=jax.ShapeDtypeStruct((m, d), F32),
        compiler_params=_cparams(1),
        name="swiglu_ple_ln",
    )(x, p, w_g, w_u, w_d, w_pg, w_pp, g, b)


def _row_tile(m):
    for tm in (1024, 512, 256, 128):
        if m % tm == 0:
            return tm
    raise ValueError(f"row count {m} is not a multiple of 128")


def kernel(x_prompt, x_sample, p_prompt, p_sample, cache_fox_k, cache_fox_v, cache_fox_logf, cache_diff_k, cache_diff_v, w_in, b_forget, lambda_q1, lambda_k1, lambda_q2, lambda_k2, diff_norm_g, w_branch_fox, w_branch_diff, w_out, ln1_g, ln1_b, w_ffn_gate, w_ffn_up, w_ffn_down, w_ple_gate, w_ple_proj, ln2_g, ln2_b):
    batch, seq, d_model = x_prompt.shape
    dec_batch, dec_seq, _ = x_sample.shape
    depth = w_in.shape[0]
    past = cache_fox_k.shape[2]
    h_a = cache_fox_k.shape[3]
    h_b = cache_diff_k.shape[3]
    w_a = h_a * HD_A
    w_b = h_b * 2 * DH_B
    assert w_a == w_b and w_in.shape[2] == 3 * w_a + h_a + 3 * w_b + 2 * d_model
    assert HD_A == DH_B
    alpha = (2 * depth) ** 0.25
    m_p = batch * seq
    m_s = dec_batch * dec_seq
    tm_p = _row_tile(seq)
    tm_s = _row_tile(m_s)
    tq, tk = min(512, seq), min(256, seq)
    tk_dec = min(2048, past)
    assert seq % tq == 0 and tq % tk == 0 and tk % CHUNK == 0 and past % tk_dec == 0
    t_dec = past + dec_seq
    t_dec_pad = -(-t_dec // SCAN_BLOCK) * SCAN_BLOCK
    qk_scale = HD_A ** -0.5
    f_pad = 16

    xp = x_prompt.reshape(m_p, d_model)
    xs = x_sample.reshape(m_s, d_model)

    wt_in = jnp.swapaxes(w_in, 1, 2)
    kt_cache_a = jnp.transpose(cache_fox_k, (0, 1, 3, 4, 2)).reshape(depth, dec_batch, w_a, past)
    vt_cache_a = jnp.transpose(cache_fox_v, (0, 1, 3, 4, 2)).reshape(depth, dec_batch, w_a, past)
    kt_cache_b = jnp.transpose(cache_diff_k, (0, 1, 3, 4, 5, 2)).reshape(depth, dec_batch, w_b, past)
    lf_cache = jnp.swapaxes(cache_fox_logf, 2, 3).astype(F32)
    v_cache_b = cache_diff_v.reshape(depth, dec_batch, past * h_b, 2 * DH_B)

    post_w = (w_branch_fox.astype(BF16), w_branch_diff.astype(BF16), w_out.astype(BF16),
              ln1_g.reshape(depth, 1, d_model), ln1_b.reshape(depth, 1, d_model))
    ffn_w = (w_ffn_gate.astype(BF16), w_ffn_up.astype(BF16), w_ffn_down.astype(BF16),
             w_ple_gate.astype(BF16), w_ple_proj.astype(BF16),
             ln2_g.reshape(depth, 1, d_model), ln2_b.reshape(depth, 1, d_model))
    pp = p_prompt.reshape(depth, m_p, -1)
    ps = p_sample.reshape(depth, m_s, -1)

    o_f = 3 * w_a
    o_qb = o_f + h_a
    o_ga = o_qb + 3 * w_b

    stacked = (jnp.zeros((depth, batch, w_a, seq), F32), jnp.zeros((depth, batch, w_a, seq), F32),
               jnp.zeros((depth, batch, w_b, seq), F32),
               jnp.zeros((depth, batch, seq, h_b, 2 * DH_B), F32),
               jnp.zeros((depth, batch, h_a, seq), F32))
    new_s = [[] for _ in range(5)]
    for l in range(depth):
        lambda_init = 0.8 - 0.6 * math.exp(-0.3 * l)
        wt = wt_in[l]
        wt_qkv = jnp.concatenate([wt[:o_f], wt[o_qb:o_ga]], axis=0).astype(BF16)
        wt_f = jnp.pad(wt[o_f:o_qb], ((0, f_pad - h_a), (0, 0))).astype(BF16)
        b_f = jnp.pad(b_forget[l], (0, f_pad - h_a))
        wt_gate = wt[o_ga:].astype(BF16)

        (qat, ka, kat, vat, vat_b, qbt, kb, kbt, vb, vbt_b, logf_p) = _inproj_prompt(
            xp, wt_qkv, wt_f, b_f.reshape(f_pad, 1), stacked, layer=l, depth=depth, batch=batch,
            seq=seq, tm=tm_p, n_heads=h_a, q_scale=qk_scale * LOG2E)
        stacked = (kat, vat, kbt, vb, logf_p)
        (qa_s, qb_s, ka_s, va_s, kb_s, vb_s, ka_sb, va_sb, kb_sb, vb_sb, logf_s) = _inproj_decode(
            xs, wt_qkv, wt_f, b_f.reshape(1, f_pad), n_heads=h_a, q_scale=qk_scale)

        c_row_p, aug_p = _cumsum_aug(logf_p[l], LOG2E)
        lf_s = jnp.swapaxes(logf_s.reshape(dec_batch, dec_seq, h_a), 1, 2)
        lf_all = jnp.concatenate([lf_cache[l], lf_s], axis=2)
        lf_all = jnp.pad(lf_all, ((0, 0), (0, 0), (0, t_dec_pad - t_dec)))
        c_row_s = _cumsum_rows(lf_all.reshape(dec_batch * h_a, t_dec_pad), 1.0)
        c_row_s = c_row_s.reshape(dec_batch, h_a, t_dec_pad)
        c_col_s = jnp.swapaxes(c_row_s, 1, 2)

        lams = [a[l].reshape(1, DH_B) for a in (lambda_q1, lambda_k1, lambda_q2, lambda_k2)]
        g_diff = diff_norm_g[l].reshape(1, w_b)

        oa_p = _fox_prompt(qat, ka, vat_b, c_row_p, aug_p,
                           batch=batch, seq=seq, tq=tq, tk=tk, n_heads=h_a)
        ob_p = _diff_prompt(qbt, kb, vbt_b, lams, g_diff, batch=batch, seq=seq, tq=tq, tk=tk,
                            n_heads=h_b, lambda_init=lambda_init)
        oa_s = _fox_decode(qa_s, kt_cache_a, vt_cache_a, ka_sb, va_sb, c_col_s, c_row_s,
                           layer=l, tq=dec_seq, tk=tk_dec, n_heads=h_a)
        ob_s = _diff_decode(qb_s, kt_cache_b, v_cache_b, kb_sb, vb_sb, lams, g_diff,
                            layer=l, tq=dec_seq, tk=tk_dec, n_heads=h_b, lambda_init=lambda_init)

        xp = _post(xp, oa_p, ob_p, wt_gate, *post_w, layer=l, tm=tm_p, alpha=alpha)
        xp = _ffn(xp, pp, *ffn_w, layer=l, tm=tm_p, alpha=alpha)
        xs = _post(xs, oa_s, ob_s, wt_gate, *post_w, layer=l, tm=tm_s, alpha=alpha)
        xs = _ffn(xs, ps, *ffn_w, layer=l, tm=tm_s, alpha=alpha)

        for lst, r in zip(new_s, (ka_s, va_s, logf_s, kb_s, vb_s)):
            lst.append(r)

    kat, vat, kbt, vb, logf_p = stacked
    ka_s, va_s, logf_s, kb_s, vb_s = [jnp.stack(a) for a in new_s]
    fox_k_p = jnp.transpose(kat.reshape(depth, batch, h_a, HD_A, seq), (0, 1, 4, 2, 3))
    fox_v_p = jnp.transpose(vat.reshape(depth, batch, h_a, HD_A, seq), (0, 1, 4, 2, 3))
    fox_lf_p = jnp.swapaxes(logf_p, 2, 3)
    diff_k_p = jnp.transpose(kbt.reshape(depth, batch, h_b, 2, DH_B, seq), (0, 1, 5, 2, 3, 4))
    return (xp.reshape(batch, seq, d_model), xs.reshape(dec_batch, dec_seq, d_model),
            fox_k_p, fox_v_p, fox_lf_p, diff_k_p, vb,
            ka_s.reshape(depth, dec_batch, dec_seq, h_a, HD_A),
            va_s.reshape(depth, dec_batch, dec_seq, h_a, HD_A),
            logf_s.reshape(depth, dec_batch, dec_seq, h_a),
            kb_s.reshape(depth, dec_batch, dec_seq, h_b, 2, DH_B),
            vb_s.reshape(depth, dec_batch, dec_seq, h_b, 2 * DH_B))
```

```python
import functools
import math

import jax
import jax.numpy as jnp
import numpy as np
from jax import lax
from jax.experimental import pallas as pl
from jax.experimental.pallas import tpu as pltpu

F32 = jnp.float32
BF16 = jnp.bfloat16

HD_A = 64
DH_B = 64
CHUNK = 64
CHUNK_SHIFT = 6
LN_EPS = 1e-5
RMS_EPS = 1e-5
NEG_INF = -1e30
LOG2E = math.log2(math.e)
LANES = 128
VMEM_LIMIT = 56 * 1024 * 1024


def _cparams(n_axes):
    return pltpu.CompilerParams(dimension_semantics=("arbitrary",) * n_axes,
                                vmem_limit_bytes=VMEM_LIMIT)


def _const_spec(shape):
    zeros = (0,) * len(shape)
    return pl.BlockSpec(shape, lambda *_: zeros, pipeline_mode=pl.Buffered(1))


def _layer_spec(shape, layer):
    zeros = (0,) * (len(shape) - 1)
    return pl.BlockSpec((None,) + tuple(shape[1:]), lambda *_: (layer,) + zeros,
                        pipeline_mode=pl.Buffered(1))


def _dot(a, b):
    return jnp.dot(a, b, preferred_element_type=F32)


def _dot_nt(a, b):
    return lax.dot_general(a, b, (((1,), (1,)), ((), ())), preferred_element_type=F32)


def _sigmoid(x):
    return 1.0 / (1.0 + jnp.exp(-x))


def _log_sigmoid(z):
    return jnp.minimum(z, 0.0) - jnp.log1p(jnp.exp(-jnp.abs(z)))


def _layer_norm(y, g, b):
    mu = jnp.mean(y, axis=-1, keepdims=True)
    yc = y - mu
    var = jnp.mean(yc * yc, axis=-1, keepdims=True)
    return yc * lax.rsqrt(var + LN_EPS) * g + b


def _inproj_prompt_kernel(x_ref, wt_ref, wft_ref, bf_ref, *refs, width, n_heads, q_scale):
    (qat_ref, ka_ref, kat_ref, vat_ref, vatb_ref,
     qbt_ref, kb_ref, kbt_ref, vb_ref, vbtb_ref, logf_ref) = refs[-11:]
    xb = x_ref[...].astype(BF16)

    def proj_t(i):
        return _dot_nt(wt_ref[i * width:(i + 1) * width, :], xb)

    qat_ref[0] = (proj_t(0) * q_scale).astype(BF16)
    kat = proj_t(1)
    kat_ref[0, 0] = kat
    ka_ref[...] = kat.T.astype(BF16)
    vat = proj_t(2)
    vat_ref[0, 0] = vat
    vatb_ref[0] = vat.astype(BF16)
    qbt_ref[0] = (proj_t(3) * q_scale).astype(BF16)
    kbt = proj_t(4)
    kbt_ref[0, 0] = kbt
    kb_ref[...] = kbt.T.astype(BF16)
    vbt = proj_t(5)
    vbtb_ref[0] = vbt.astype(BF16)
    vb = vbt.T
    for h in range(width // LANES):
        vb_ref[0, 0, :, h, :] = vb[:, h * LANES:(h + 1) * LANES]
    z = _dot_nt(wft_ref[...], xb) + bf_ref[...]
    logf_ref[0, 0] = _log_sigmoid(z)[:n_heads, :]


def _inproj_prompt(x, wt_qkv, wt_f, b_f_col, stacked, *, layer, depth, batch, seq, tm, n_heads,
                   q_scale):
    m, d = x.shape
    width = wt_qkv.shape[0] // 6
    nt = seq // tm
    row = lambda i: (i, 0)
    tr = lambda i: (i // nt, 0, i % nt)
    tr_l = lambda i: (layer, i // nt, 0, i % nt)
    t_blk = pl.BlockSpec((1, width, tm), tr)
    t_blk_l = pl.BlockSpec((1, 1, width, tm), tr_l)
    n_blk = pl.BlockSpec((tm, width), row)
    t_f32 = jax.ShapeDtypeStruct((depth, batch, width, seq), F32)
    t_b16 = jax.ShapeDtypeStruct((batch, width, seq), BF16)
    n_b16 = jax.ShapeDtypeStruct((m, width), BF16)
    vb_heads = width // LANES
    out_specs = [t_blk, n_blk, t_blk_l, t_blk_l, t_blk, t_blk, n_blk, t_blk_l,
                 pl.BlockSpec((1, 1, tm, vb_heads, LANES),
                              lambda i: (layer, i // nt, i % nt, 0, 0)),
                 t_blk, pl.BlockSpec((1, 1, n_heads, tm), tr_l)]
    out_shape = [t_b16, n_b16, t_f32, t_f32, t_b16, t_b16, n_b16, t_f32,
                 jax.ShapeDtypeStruct((depth, batch, seq, vb_heads, LANES), F32),
                 t_b16, jax.ShapeDtypeStruct((depth, batch, n_heads, seq), F32)]
    stacked_outputs = (2, 3, 7, 8, 10)
    in_specs = [pl.BlockSpec((tm, d), row), _const_spec(wt_qkv.shape), _const_spec(wt_f.shape),
                _const_spec(b_f_col.shape)]
    args = [x, wt_qkv, wt_f, b_f_col]
    aliases = {len(args) + i: o for i, o in enumerate(stacked_outputs)}
    in_specs += [pl.BlockSpec(memory_space=pl.ANY)] * len(stacked)
    args += list(stacked)
    return pl.pallas_call(
        functools.partial(_inproj_prompt_kernel, width=width, n_heads=n_heads, q_scale=q_scale),
        grid=(m // tm,),
        in_specs=in_specs,
        out_specs=out_specs,
        out_shape=out_shape,
        input_output_aliases=aliases,
        compiler_params=_cparams(1),
        name="inproj_prompt",
    )(*args)


def _inproj_decode_kernel(x_ref, wt_ref, wft_ref, bf_ref,
                          qa_ref, qb_ref, ka_ref, va_ref, kb_ref, vb_ref,
                          kab_ref, vab_ref, kbb_ref, vbb_ref, logf_ref, *, width, n_heads, q_scale):
    xb = x_ref[...].astype(BF16)

    def proj(i):
        return _dot_nt(xb, wt_ref[i * width:(i + 1) * width, :])

    qa_ref[...] = (proj(0) * q_scale).astype(BF16)
    qb_ref[...] = (proj(3) * q_scale).astype(BF16)
    for i, (full_ref, half_ref) in ((1, (ka_ref, kab_ref)), (2, (va_ref, vab_ref)),
                                    (4, (kb_ref, kbb_ref)), (5, (vb_ref, vbb_ref))):
        val = proj(i)
        full_ref[...] = val
        half_ref[...] = val.astype(BF16)
    z = _dot_nt(xb, wft_ref[...]) + bf_ref[...]
    logf_ref[...] = _log_sigmoid(z)[:, :n_heads]


def _inproj_decode(x, wt_qkv, wt_f, b_f_row, *, n_heads, q_scale):
    m, d = x.shape
    width = wt_qkv.shape[0] // 6
    out_shape = ([jax.ShapeDtypeStruct((m, width), BF16)] * 2
                 + [jax.ShapeDtypeStruct((m, width), F32)] * 4
                 + [jax.ShapeDtypeStruct((m, width), BF16)] * 4
                 + [jax.ShapeDtypeStruct((m, n_heads), F32)])
    return pl.pallas_call(
        functools.partial(_inproj_decode_kernel, width=width, n_heads=n_heads, q_scale=q_scale),
        out_shape=out_shape,
        compiler_params=pltpu.CompilerParams(vmem_limit_bytes=VMEM_LIMIT),
        name="inproj_decode",
    )(x, wt_qkv, wt_f, b_f_row)


SCAN_BLOCK = 256


def _split3(a):
    a1 = a.astype(BF16)
    r1 = a - a1.astype(F32)
    a2 = r1.astype(BF16)
    a3 = (r1 - a2.astype(F32)).astype(BF16)
    return a1, a2, a3


def _split3_const(x):
    pieces, rest = [], np.float32(x)
    for _ in range(3):
        p = np.float32(np.asarray(rest, dtype=jnp.bfloat16))
        pieces.append(float(p))
        rest = np.float32(rest - p)
    return pieces


def _cumsum_kernel(lf_ref, c_ref, *, n_blk, scale):
    rows = lf_ref.shape[0]
    r = lax.broadcasted_iota(jnp.int32, (SCAN_BLOCK, SCAN_BLOCK), 0)
    c = lax.broadcasted_iota(jnp.int32, (SCAN_BLOCK, SCAN_BLOCK), 1)
    upper = jnp.where(r <= c, 1.0, 0.0).astype(BF16)
    carry = jnp.zeros((rows, 1), F32)
    for j in range(n_blk):
        sl = slice(j * SCAN_BLOCK, (j + 1) * SCAN_BLOCK)
        a1, a2, a3 = _split3(lf_ref[:, sl])
        blk = _dot(a1, upper) + _dot(a2, upper) + _dot(a3, upper) + carry
        c_ref[:, sl] = blk * scale
        carry = blk[:, SCAN_BLOCK - 1:SCAN_BLOCK]


def _cumsum_rows(lf_rows, scale):
    rows, t = lf_rows.shape
    assert t % SCAN_BLOCK == 0
    return pl.pallas_call(
        functools.partial(_cumsum_kernel, n_blk=t // SCAN_BLOCK, scale=scale),
        out_shape=jax.ShapeDtypeStruct((rows, t), F32),
        compiler_params=pltpu.CompilerParams(vmem_limit_bytes=VMEM_LIMIT),
        name="logf_cumsum",
    )(lf_rows)


AUG_PIECES = 3


def _aug_lane(h, piece):
    return AUG_PIECES * h + piece


def _cumsum_aug_kernel(lf_rows_ref, lf_cols_ref, place_ref, c_ref, aug_ref, *, n_blk, scale):
    rows = lf_rows_ref.shape[0]
    batch, _, width = aug_ref.shape
    r = lax.broadcasted_iota(jnp.int32, (SCAN_BLOCK, SCAN_BLOCK), 0)
    c = lax.broadcasted_iota(jnp.int32, (SCAN_BLOCK, SCAN_BLOCK), 1)
    upper = jnp.where(r <= c, 1.0, 0.0).astype(BF16)
    lower = jnp.where(c <= r, 1.0, 0.0).astype(BF16)
    carry_r = jnp.zeros((rows, 1), F32)
    carry_c = jnp.zeros((1, rows), F32)
    for j in range(n_blk):
        sl = slice(j * SCAN_BLOCK, (j + 1) * SCAN_BLOCK)
        a1, a2, a3 = _split3(lf_rows_ref[:, sl])
        blk_r = _dot(a1, upper) + _dot(a2, upper) + _dot(a3, upper) + carry_r
        c_ref[:, sl] = blk_r * scale
        carry_r = blk_r[:, SCAN_BLOCK - 1:SCAN_BLOCK]
        b1, b2, b3 = _split3(lf_cols_ref[sl, :])
        blk_c = _dot(lower, b1) + _dot(lower, b2) + _dot(lower, b3) + carry_c
        carry_c = blk_c[SCAN_BLOCK - 1:SCAN_BLOCK, :]
        pieces = jnp.concatenate(_split3(blk_c * scale), axis=1)
        placed = _dot(pieces, place_ref[...]).astype(BF16)
        for b in range(batch):
            aug_ref[b, sl, :] = placed[:, b * width:(b + 1) * width]


def _placement_matrix(batch, n_heads, width):
    rows = batch * n_heads
    place = np.zeros((AUG_PIECES * rows, batch * width), np.float32)
    for b in range(batch):
        for h in range(n_heads):
            for p in range(AUG_PIECES):
                place[p * rows + b * n_heads + h, b * width + _aug_lane(h, p)] = 1.0
    return jnp.asarray(place, BF16)


def _cumsum_aug(lf, scale):
    batch, n_heads, t = lf.shape
    assert t % SCAN_BLOCK == 0 and AUG_PIECES * n_heads <= LANES
    width = LANES
    rows = batch * n_heads
    c_rows, aug = pl.pallas_call(
        functools.partial(_cumsum_aug_kernel, n_blk=t // SCAN_BLOCK, scale=scale),
        out_shape=[jax.ShapeDtypeStruct((rows, t), F32),
                   jax.ShapeDtypeStruct((batch, t, width), BF16)],
        compiler_params=pltpu.CompilerParams(vmem_limit_bytes=VMEM_LIMIT),
        name="logf_cumsum_aug",
    )(lf.reshape(rows, t), jnp.transpose(lf, (2, 0, 1)).reshape(t, rows),
      _placement_matrix(batch, n_heads, width))
    return c_rows.reshape(batch, n_heads, t), aug


def _lambda_scalar(lq1, lk1, lq2, lk2, lambda_init):
    return (jnp.exp(jnp.sum(lq1 * lk1, axis=1, keepdims=True))
            - jnp.exp(jnp.sum(lq2 * lk2, axis=1, keepdims=True)) + lambda_init)


def _head_rms(o, g_row, lambda_init):
    o = o * lax.rsqrt(jnp.mean(o * o, axis=-1, keepdims=True) + RMS_EPS)
    return o * g_row * (1.0 - lambda_init)


def _alibi_slope(h, n_heads):
    return 2.0 ** (-8.0 * (h + 1) / n_heads)


def _sublane_half_masks(cols):
    row = lax.broadcasted_iota(jnp.int32, (LANES, cols), 0)
    lo = jnp.where(row < HD_A, 1.0, 0.0)
    return lo, 1.0 - lo


SUM_ROWS = 16


def _with_sum_rows(v_t):
    return jnp.concatenate([v_t, jnp.ones((SUM_ROWS, v_t.shape[1]), BF16)], axis=0)


def _flash_step_refs(i, t, shift, v_aug, m_ref, acc_ref, qs=slice(None)):
    m = m_ref[i, :, qs]
    m_new = jnp.maximum(m, jnp.max(t, axis=0, keepdims=True) + shift)
    p = jnp.exp2(t - (m_new - shift)).astype(BF16)
    acc_ref[i, :, qs] = jnp.exp2(m - m_new) * acc_ref[i, :, qs] + _dot(v_aug, p)
    m_ref[i, :, qs] = m_new


def _flash_init_refs(m_ref, acc_ref):
    m_ref[...] = jnp.full(m_ref.shape, NEG_INF, F32)
    acc_ref[...] = jnp.zeros(acc_ref.shape, F32)


def _flash_result(i, d, acc_ref):
    acc = acc_ref[i]
    return acc[:d] / acc[d:d + 1]


def _stream_scratch(n_streams, d, tq):
    return [pltpu.VMEM((n_streams, 2 * LANES, tq), BF16), pltpu.VMEM((n_streams, 1, tq), F32),
            pltpu.VMEM((n_streams, d + SUM_ROWS, tq), F32)]


def _fox_prompt_kernel(qt_ref, k_ref, vt_ref, crow_ref, aug_ref, o_ref, qm_ref, m_ref, acc_ref,
                       *, tq, tk, n_heads):
    qi = pl.program_id(1)
    n_diag = tq // tk
    lo, hi = _sublane_half_masks(tq)
    row = lax.broadcasted_iota(jnp.int32, (LANES, tq), 0)
    for h in range(n_heads):
        blk = slice((h // 2) * LANES, (h // 2 + 1) * LANES)
        qm_ref[h, :LANES] = (qt_ref[0, blk, :].astype(F32) * (lo, hi)[h % 2]).astype(BF16)
        picks = (row >= _aug_lane(h, 0)) & (row < _aug_lane(h, AUG_PIECES))
        qm_ref[h, LANES:] = jnp.where(picks, -1.0, 0.0).astype(BF16)
    _flash_init_refs(m_ref, acc_ref)

    def tile(ks, mask, qs=slice(None)):
        v_all = vt_ref[0, :, pl.ds(ks, tk)]
        scores = []
        for h in range(n_heads):
            blk = slice((h // 2) * LANES, (h // 2 + 1) * LANES)
            k_aug = jnp.concatenate([k_ref[pl.ds(ks, tk), blk], aug_ref[0, pl.ds(ks, tk), :]],
                                    axis=1)
            t = _dot(k_aug, qm_ref[h, :, qs])
            if mask is not None:
                t = jnp.where(mask, t, NEG_INF)
            scores.append(t)
        for h in range(n_heads):
            _flash_step_refs(h, scores[h], crow_ref[0, h:h + 1, qs],
                             _with_sum_rows(v_all[h * HD_A:(h + 1) * HD_A]), m_ref, acc_ref, qs)

    def full_tile(j, carry):
        tile(pl.multiple_of(j * tk, tk), None)
        return carry

    lax.fori_loop(0, qi * n_diag, full_tile, 0)
    for d in range(n_diag):
        nq = tq - d * tk
        r = lax.broadcasted_iota(jnp.int32, (tk, nq), 0)
        c = lax.broadcasted_iota(jnp.int32, (tk, nq), 1)
        tile(pl.multiple_of(qi * tq + d * tk, tk), r <= c, slice(d * tk, tq))
    for pair in range(n_heads // 2):
        blk = slice(pair * LANES, (pair + 1) * LANES)
        o_t = jnp.concatenate([_flash_result(2 * pair, HD_A, acc_ref),
                               _flash_result(2 * pair + 1, HD_A, acc_ref)], axis=0)
        o_ref[:, blk] = o_t.T.astype(BF16)


def _fox_prompt(qt, k, vt, c_row, aug, *, batch, seq, tq, tk, n_heads):
    width = n_heads * HD_A
    nq = seq // tq
    return pl.pallas_call(
        functools.partial(_fox_prompt_kernel, tq=tq, tk=tk, n_heads=n_heads),
        grid=(batch, nq),
        in_specs=[pl.BlockSpec((1, width, tq), lambda b, i: (b, 0, i)),
                  pl.BlockSpec((seq, width), lambda b, i: (b, 0)),
                  pl.BlockSpec((1, width, seq), lambda b, i: (b, 0, 0)),
                  pl.BlockSpec((1, n_heads, tq), lambda b, i: (b, 0, i)),
                  pl.BlockSpec((1, seq, LANES), lambda b, i: (b, 0, 0))],
        out_specs=pl.BlockSpec((tq, width), lambda b, i: (b * nq + i, 0)),
        out_shape=jax.ShapeDtypeStruct((batch * seq, width), BF16),
        scratch_shapes=_stream_scratch(n_heads, HD_A, tq),
        compiler_params=_cparams(2),
        name="fox_prompt",
    )(qt, k, vt, c_row, aug)


def _diff_prompt_kernel(qt_ref, k_ref, vt_ref, lq1_ref, lk1_ref, lq2_ref, lk2_ref, g_ref, o_ref,
                        qm_ref, m_ref, acc_ref, *, tq, tk, n_heads, lambda_init):
    qi = pl.program_id(1)
    n_diag = tq // tk
    lo, hi = _sublane_half_masks(tq)
    row = lax.broadcasted_iota(jnp.int32, (LANES, tq), 0)
    for i in range(2 * n_heads):
        blk = slice((i // 2) * LANES, (i // 2 + 1) * LANES)
        qm_ref[i, :LANES] = (qt_ref[0, blk, :].astype(F32) * (lo, hi)[i % 2]).astype(BF16)
        s1, s2, s3 = _split3_const(_alibi_slope(i // 2, n_heads) * LOG2E)
        slope_rows = jnp.where(row == 0, s1, jnp.where(row == 1, s2, jnp.where(row == 2, s3, 0.0)))
        qm_ref[i, LANES:] = slope_rows.astype(BF16)
    _flash_init_refs(m_ref, acc_ref)
    assert tk <= 256
    lane = lax.broadcasted_iota(jnp.int32, (tk, LANES), 1)
    key_off = lax.broadcasted_iota(jnp.int32, (tk, LANES), 0).astype(F32)
    key_off = jnp.where(lane < AUG_PIECES, key_off, 0.0).astype(BF16)
    q_off = lax.broadcasted_iota(jnp.int32, (1, tq), 1).astype(F32)

    def tile(ks, scores_of, shift_of, qs=slice(None)):
        k_all = k_ref[pl.ds(ks, tk), :]
        v_all = vt_ref[0, :, pl.ds(ks, tk)]
        scores = [scores_of(k_all[:, (i // 2) * LANES:(i // 2 + 1) * LANES], i)
                  for i in range(2 * n_heads)]
        for i in range(2 * n_heads):
            h = i // 2
            _flash_step_refs(i, scores[i], shift_of(_alibi_slope(h, n_heads) * LOG2E),
                             _with_sum_rows(v_all[h * LANES:(h + 1) * LANES]), m_ref, acc_ref, qs)

    def full_tile(j, carry):
        off = (j * tk - qi * tq).astype(F32) - q_off
        tile(pl.multiple_of(j * tk, tk),
             lambda k2, i: _dot(jnp.concatenate([k2, key_off], axis=1), qm_ref[i]),
             lambda s2: s2 * off)
        return carry

    lax.fori_loop(0, qi * n_diag, full_tile, 0)
    for d in range(n_diag):
        qs = slice(d * tk, tq)
        r = lax.broadcasted_iota(jnp.int32, (tk, tq - d * tk), 0)
        c = lax.broadcasted_iota(jnp.int32, (tk, tq - d * tk), 1)
        dist = jnp.abs(r - c).astype(F32)
        visible = (lax.shift_right_logical(r, CHUNK_SHIFT)
                   <= lax.shift_right_logical(c, CHUNK_SHIFT))

        def diag_scores(k2, i):
            slope2 = _alibi_slope(i // 2, n_heads) * LOG2E
            return jnp.where(visible, _dot(k2, qm_ref[i, :LANES, qs]) - slope2 * dist, NEG_INF)

        tile(pl.multiple_of(qi * tq + d * tk, tk), diag_scores, lambda s2: 0.0, qs)
    lam = _lambda_scalar(lq1_ref[...], lk1_ref[...], lq2_ref[...], lk2_ref[...], lambda_init)
    for h in range(n_heads):
        blk = slice(h * LANES, (h + 1) * LANES)
        o = (_flash_result(2 * h, LANES, acc_ref) - lam * _flash_result(2 * h + 1, LANES, acc_ref)).T
        o_ref[:, blk] = _head_rms(o, g_ref[:, blk], lambda_init).astype(BF16)


def _diff_prompt(qt, k, vt, lams, g, *, batch, seq, tq, tk, n_heads, lambda_init):
    width = n_heads * 2 * DH_B
    nq = seq // tq
    return pl.pallas_call(
        functools.partial(_diff_prompt_kernel, tq=tq, tk=tk, n_heads=n_heads,
                          lambda_init=lambda_init),
        grid=(batch, nq),
        in_specs=[pl.BlockSpec((1, width, tq), lambda b, i: (b, 0, i)),
                  pl.BlockSpec((seq, width), lambda b, i: (b, 0)),
                  pl.BlockSpec((1, width, seq), lambda b, i: (b, 0, 0))]
                 + [_const_spec(a.shape) for a in lams] + [_const_spec(g.shape)],
        out_specs=pl.BlockSpec((tq, width), lambda b, i: (b * nq + i, 0)),
        out_shape=jax.ShapeDtypeStruct((batch * seq, width), BF16),
        scratch_shapes=_stream_scratch(2 * n_heads, LANES, tq),
        compiler_params=_cparams(2),
        name="diff_prompt",
    )(qt, k, vt, *lams, g)


def _lane_half_masks(rows):
    lane = lax.broadcasted_iota(jnp.int32, (rows, LANES), 1)
    lo = jnp.where(lane < HD_A, 1.0, 0.0)
    return lo, 1.0 - lo


def _masked_q(q2, mask):
    return (q2.astype(F32) * mask).astype(BF16)


def _flash_init(m_ref, l_ref, acc_ref):
    m_ref[...] = jnp.full(m_ref.shape, NEG_INF, F32)
    l_ref[...] = jnp.zeros(l_ref.shape, F32)
    acc_ref[...] = jnp.zeros(acc_ref.shape, F32)


def _flash_step(s, pv, m_ref, l_ref, acc_ref):
    m_prev = m_ref[...]
    m_new = jnp.maximum(m_prev, jnp.max(s, axis=1, keepdims=True))
    alpha = jnp.exp(m_prev - m_new)
    p = jnp.exp(s - m_new)
    l_ref[...] = alpha * l_ref[...] + jnp.sum(p, axis=1, keepdims=True)
    acc_ref[...] = alpha * acc_ref[...] + pv(p.astype(BF16))
    m_ref[...] = m_new


def _fox_decode_kernel(q_ref, kc_ref, vc_ref, kn_ref, vn_ref, ccol_ref, crow_ref, o_ref,
                       m_ref, l_ref, acc_ref, *, tq, tk, n_kv, n_heads):
    j = pl.program_id(1)
    lo, hi = _lane_half_masks(tq)

    @pl.when(j == 0)
    def _():
        _flash_init(m_ref, l_ref, acc_ref)

    def head_loop(scores_of, pv_of, bias_of, mask):
        pending = []
        for pair in range(n_heads // 2):
            blk = slice(pair * LANES, (pair + 1) * LANES)
            q2 = q_ref[:, blk]
            scores = scores_of(blk)
            pv = pv_of(blk)
            for half, hm in enumerate((lo, hi)):
                s = scores(_masked_q(q2, hm)) + bias_of(2 * pair + half)
                if mask is not None:
                    s = jnp.where(mask, s, NEG_INF)
                pending.append((s, pv))
        for h, (s, pv) in enumerate(pending):
            _flash_step(s, pv, m_ref.at[h], l_ref.at[h], acc_ref.at[h])

    def cache_tile():
        ks = pl.multiple_of(j * tk, tk)

        def scores_of(blk):
            kt = kc_ref[0, 0, blk, :].astype(BF16)
            return lambda qm: _dot(qm, kt)

        def pv_of(blk):
            vt = vc_ref[0, 0, blk, :].astype(BF16)
            return lambda p: _dot_nt(p, vt)

        head_loop(scores_of, pv_of,
                  lambda h: ccol_ref[0, :, h:h + 1] - crow_ref[0, h:h + 1, pl.ds(ks, tk)], None)

    cache_tile()

    @pl.when(j == n_kv - 1)
    def _():
        past = n_kv * tk
        q_pos = lax.broadcasted_iota(jnp.int32, (tq, tq), 0)
        k_pos = lax.broadcasted_iota(jnp.int32, (tq, tq), 1)
        head_loop(lambda blk: (lambda qm: _dot_nt(qm, kn_ref[:, blk])),
                  lambda blk: (lambda p: _dot(p, vn_ref[:, blk])),
                  lambda h: ccol_ref[0, :, h:h + 1] - crow_ref[0, h:h + 1, past:past + tq],
                  k_pos <= q_pos)
        for pair in range(n_heads // 2):
            blk = slice(pair * LANES, (pair + 1) * LANES)
            o0 = acc_ref[2 * pair] / l_ref[2 * pair]
            o1 = acc_ref[2 * pair + 1] / l_ref[2 * pair + 1]
            o_ref[:, blk] = (o0 * lo + o1 * hi).astype(BF16)


def _fox_decode(q, kt_cache, vt_cache, k_new, v_new, c_col, c_row, *, layer, tq, tk, n_heads):
    _, batch, width, past = kt_cache.shape
    n_kv = past // tk
    new_blk = pl.BlockSpec((tq, width), lambda b, j: (b, 0))
    cache_blk = pl.BlockSpec((1, 1, width, tk),
                             lambda b, j: (layer, b, 0, j))
    return pl.pallas_call(
        functools.partial(_fox_decode_kernel, tq=tq, tk=tk, n_kv=n_kv, n_heads=n_heads),
        grid=(batch, n_kv),
        in_specs=[new_blk, cache_blk, cache_blk, new_blk, new_blk,
                  pl.BlockSpec((1, tq, n_heads), lambda b, j: (b, past // tq, 0)),
                  pl.BlockSpec((1, n_heads, c_row.shape[2]), lambda b, j: (b, 0, 0))],
        out_specs=new_blk,
        out_shape=jax.ShapeDtypeStruct((batch * tq, width), BF16),
        scratch_shapes=[pltpu.VMEM((n_heads, tq, 1), F32), pltpu.VMEM((n_heads, tq, 1), F32),
                        pltpu.VMEM((n_heads, tq, LANES), F32)],
        compiler_params=_cparams(2),
        name="fox_decode",
    )(q, kt_cache, vt_cache, k_new, v_new, c_col, c_row)


def _diff_decode_kernel(q_ref, kc_ref, vc_ref, kn_ref, vn_ref, lq1_ref, lk1_ref, lq2_ref, lk2_ref,
                        g_ref, o_ref, m_ref, l_ref, acc_ref,
                        *, tq, tk, n_kv, n_heads, lambda_init):
    j = pl.program_id(1)
    lo, hi = _lane_half_masks(tq)
    past = n_kv * tk
    q_abs = (past + lax.broadcasted_iota(jnp.int32, (tq, 1), 0)).astype(F32)

    @pl.when(j == 0)
    def _():
        _flash_init(m_ref, l_ref, acc_ref)

    def head_loop(scores_of, pv_of, dist):
        pending = []
        for h in range(n_heads):
            blk = slice(h * LANES, (h + 1) * LANES)
            slope = _alibi_slope(h, n_heads)
            q2 = q_ref[:, blk]
            scores = scores_of(blk)
            pv = pv_of(h)
            for hm in (lo, hi):
                pending.append((scores(_masked_q(q2, hm)) - slope * dist, pv))
        for idx, (s, pv) in enumerate(pending):
            _flash_step(s, pv, m_ref.at[idx], l_ref.at[idx], acc_ref.at[idx])

    def cache_tile():
        k_abs = (j * tk + lax.broadcasted_iota(jnp.int32, (1, tk), 1)).astype(F32)

        def scores_of(blk):
            kt = kc_ref[0, 0, blk, :].astype(BF16)
            return lambda qm: _dot(qm, kt)

        def pv_of(h):
            v2 = vc_ref[0, 0, pl.ds(h, tk, stride=n_heads), :].astype(BF16)
            return lambda p: _dot(p, v2)

        head_loop(scores_of, pv_of, jnp.abs(q_abs - k_abs))

    cache_tile()

    @pl.when(j == n_kv - 1)
    def _():
        assert past % CHUNK == 0 and tq <= CHUNK
        k_abs = (past + lax.broadcasted_iota(jnp.int32, (1, tq), 1)).astype(F32)
        head_loop(lambda blk: (lambda qm: _dot_nt(qm, kn_ref[:, blk])),
                  lambda h: (lambda p: _dot(p, vn_ref[:, h * LANES:(h + 1) * LANES])),
                  jnp.abs(q_abs - k_abs))
        lam = _lambda_scalar(lq1_ref[...], lk1_ref[...], lq2_ref[...], lk2_ref[...], lambda_init)
        for h in range(n_heads):
            blk = slice(h * LANES, (h + 1) * LANES)
            o = acc_ref[2 * h] / l_ref[2 * h] - lam * (acc_ref[2 * h + 1] / l_ref[2 * h + 1])
            o_ref[:, blk] = _head_rms(o, g_ref[:, blk], lambda_init).astype(BF16)


def _diff_decode(q, kt_cache, v_cache, k_new, v_new, lams, g, *, layer, tq, tk, n_heads,
                 lambda_init):
    _, batch, width, past = kt_cache.shape
    n_kv = past // tk
    new_blk = pl.BlockSpec((tq, width), lambda b, j: (b, 0))
    return pl.pallas_call(
        functools.partial(_diff_decode_kernel, tq=tq, tk=tk, n_kv=n_kv, n_heads=n_heads,
                          lambda_init=lambda_init),
        grid=(batch, n_kv),
        in_specs=[new_blk,
                  pl.BlockSpec((1, 1, width, tk),
                               lambda b, j: (layer, b, 0, j)),
                  pl.BlockSpec((1, 1, tk * n_heads, LANES),
                               lambda b, j: (layer, b, j, 0)),
                  new_blk, new_blk]
                 + [_const_spec(a.shape) for a in lams] + [_const_spec(g.shape)],
        out_specs=new_blk,
        out_shape=jax.ShapeDtypeStruct((batch * tq, width), BF16),
        scratch_shapes=[pltpu.VMEM((2 * n_heads, tq, 1), F32),
                        pltpu.VMEM((2 * n_heads, tq, 1), F32),
                        pltpu.VMEM((2 * n_heads, tq, LANES), F32)],
        compiler_params=_cparams(2),
        name="diff_decode",
    )(q, kt_cache, v_cache, k_new, v_new, *lams, g)


def _post_kernel(x_ref, oa_ref, ob_ref, wgt_ref, wba_ref, wbb_ref, wo_ref, g_ref, b_ref, o_ref,
                 *, alpha):
    x = x_ref[...]
    xb = x.astype(BF16)
    d = x.shape[1]
    merged = _sigmoid(_dot_nt(xb, wgt_ref[:d, :])) * _dot(oa_ref[...], wba_ref[...])
    merged = merged + _sigmoid(_dot_nt(xb, wgt_ref[d:, :])) * _dot(ob_ref[...], wbb_ref[...])
    y = alpha * x + _dot(merged.astype(BF16), wo_ref[...])
    o_ref[...] = _layer_norm(y, g_ref[...], b_ref[...])


def _post(x, oa, ob, wt_gate, w_ba, w_bb, w_o, g, b, *, layer, tm, alpha):
    m, d = x.shape
    row = lambda i: (i, 0)
    return pl.pallas_call(
        functools.partial(_post_kernel, alpha=alpha),
        grid=(m // tm,),
        in_specs=[pl.BlockSpec((tm, d), row), pl.BlockSpec((tm, oa.shape[1]), row),
                  pl.BlockSpec((tm, ob.shape[1]), row), _const_spec(wt_gate.shape)]
                 + [_layer_spec(a.shape, layer) for a in (w_ba, w_bb, w_o, g, b)],
        out_specs=pl.BlockSpec((tm, d), row),
        out_shape=jax.ShapeDtypeStruct((m, d), F32),
        compiler_params=_cparams(1),
        name="merge_outproj_ln",
    )(x, oa, ob, wt_gate, w_ba, w_bb, w_o, g, b)


def _ffn_kernel(x_ref, p_ref, wg_ref, wu_ref, wd_ref, wpg_ref, wpp_ref, g_ref, b_ref, o_ref,
                *, alpha):
    x = x_ref[...]
    xb = x.astype(BF16)
    d_ff = wg_ref.shape[1]
    cut = (d_ff // 512) * 256
    y = alpha * x
    for lo_c, hi_c in ((0, cut), (cut, d_ff)):
        hg = _dot(xb, wg_ref[:, lo_c:hi_c])
        hidden = (hg * _sigmoid(hg) * _dot(xb, wu_ref[:, lo_c:hi_c])).astype(BF16)
        y = y + _dot(hidden, wd_ref[lo_c:hi_c, :])
    ple = _sigmoid(_dot(xb, wpg_ref[...])) * _dot(p_ref[...].astype(BF16), wpp_ref[...])
    o_ref[...] = _layer_norm(y + ple, g_ref[...], b_ref[...])


def _ffn(x, p, w_g, w_u, w_d, w_pg, w_pp, g, b, *, layer, tm, alpha):
    m, d = x.shape
    row = lambda i: (i, 0)
    return pl.pallas_call(
        functools.partial(_ffn_kernel, alpha=alpha),
        grid=(m // tm,),
        in_specs=[pl.BlockSpec((tm, d), row),
                  pl.BlockSpec((None, tm, p.shape[2]), lambda i: (layer, i, 0))]
                 + [_layer_spec(a.shape, layer) for a in (w_g, w_u, w_d, w_pg, w_pp, g, b)],
        out_specs=pl.BlockSpec((tm, d), row),
        out_shape=jax.ShapeDtypeStruct((m, d), F32),
        compiler_params=_cparams(1),
        name="swiglu_ple_ln",
    )(x, p, w_g, w_u, w_d, w_pg, w_pp, g, b)


def _row_tile(m):
    for tm in (1024, 512, 256, 128):
        if m % tm == 0:
            return tm
    raise ValueError(f"row count {m} is not a multiple of 128")


def kernel(x_prompt, x_sample, p_prompt, p_sample, cache_fox_k, cache_fox_v, cache_fox_logf, cache_diff_k, cache_diff_v, w_in, b_forget, lambda_q1, lambda_k1, lambda_q2, lambda_k2, diff_norm_g, w_branch_fox, w_branch_diff, w_out, ln1_g, ln1_b, w_ffn_gate, w_ffn_up, w_ffn_down, w_ple_gate, w_ple_proj, ln2_g, ln2_b):
    batch, seq, d_model = x_prompt.shape
    dec_batch, dec_seq, _ = x_sample.shape
    depth = w_in.shape[0]
    past = cache_fox_k.shape[2]
    h_a = cache_fox_k.shape[3]
    h_b = cache_diff_k.shape[3]
    w_a = h_a * HD_A
    w_b = h_b * 2 * DH_B
    assert w_a == w_b and w_in.shape[2] == 3 * w_a + h_a + 3 * w_b + 2 * d_model
    assert HD_A == DH_B
    alpha = (2 * depth) ** 0.25
    m_p = batch * seq
    m_s = dec_batch * dec_seq
    tm_p = _row_tile(seq)
    tm_s = _row_tile(m_s)
    tq, tk = min(512, seq), min(256, seq)
    tk_dec = min(2048, past)
    assert seq % tq == 0 and tq % tk == 0 and tk % CHUNK == 0 and past % tk_dec == 0
    t_dec = past + dec_seq
    t_dec_pad = -(-t_dec // SCAN_BLOCK) * SCAN_BLOCK
    qk_scale = HD_A ** -0.5
    f_pad = 16

    xp = x_prompt.reshape(m_p, d_model)
    xs = x_sample.reshape(m_s, d_model)

    wt_in = jnp.swapaxes(w_in, 1, 2)
    kt_cache_a = jnp.transpose(cache_fox_k, (0, 1, 3, 4, 2)).reshape(depth, dec_batch, w_a, past)
    vt_cache_a = jnp.transpose(cache_fox_v, (0, 1, 3, 4, 2)).reshape(depth, dec_batch, w_a, past)
    kt_cache_b = jnp.transpose(cache_diff_k, (0, 1, 3, 4, 5, 2)).reshape(depth, dec_batch, w_b, past)
    lf_cache = jnp.swapaxes(cache_fox_logf, 2, 3).astype(F32)
    v_cache_b = cache_diff_v.reshape(depth, dec_batch, past * h_b, 2 * DH_B)

    post_w = (w_branch_fox.astype(BF16), w_branch_diff.astype(BF16), w_out.astype(BF16),
              ln1_g.reshape(depth, 1, d_model), ln1_b.reshape(depth, 1, d_model))
    ffn_w = (w_ffn_gate.astype(BF16), w_ffn_up.astype(BF16), w_ffn_down.astype(BF16),
             w_ple_gate.astype(BF16), w_ple_proj.astype(BF16),
             ln2_g.reshape(depth, 1, d_model), ln2_b.reshape(depth, 1, d_model))
    pp = p_prompt.reshape(depth, m_p, -1)
    ps = p_sample.reshape(depth, m_s, -1)

    o_f = 3 * w_a
    o_qb = o_f + h_a
    o_ga = o_qb + 3 * w_b

    stacked = (jnp.zeros((depth, batch, w_a, seq), F32), jnp.zeros((depth, batch, w_a, seq), F32),
               jnp.zeros((depth, batch, w_b, seq), F32),
               jnp.zeros((depth, batch, seq, h_b, 2 * DH_B), F32),
               jnp.zeros((depth, batch, h_a, seq), F32))
    new_s = [[] for _ in range(5)]
    for l in range(depth):
        lambda_init = 0.8 - 0.6 * math.exp(-0.3 * l)
        wt = wt_in[l]
        wt_qkv = jnp.concatenate([wt[:o_f], wt[o_qb:o_ga]], axis=0).astype(BF16)
        wt_f = jnp.pad(wt[o_f:o_qb], ((0, f_pad - h_a), (0, 0))).astype(BF16)
        b_f = jnp.pad(b_forget[l], (0, f_pad - h_a))
        wt_gate = wt[o_ga:].astype(BF16)

        (qat, ka, kat, vat, vat_b, qbt, kb, kbt, vb, vbt_b, logf_p) = _inproj_prompt(
            xp, wt_qkv, wt_f, b_f.reshape(f_pad, 1), stacked, layer=l, depth=depth, batch=batch,
            seq=seq, tm=tm_p, n_heads=h_a, q_scale=qk_scale * LOG2E)
        stacked = (kat, vat, kbt, vb, logf_p)
        (qa_s, qb_s, ka_s, va_s, kb_s, vb_s, ka_sb, va_sb, kb_sb, vb_sb, logf_s) = _inproj_decode(
            xs, wt_qkv, wt_f, b_f.reshape(1, f_pad), n_heads=h_a, q_scale=qk_scale)

        c_row_p, aug_p = _cumsum_aug(logf_p[l], LOG2E)
        lf_s = jnp.swapaxes(logf_s.reshape(dec_batch, dec_seq, h_a), 1, 2)
        lf_all = jnp.concatenate([lf_cache[l], lf_s], axis=2)
        lf_all = jnp.pad(lf_all, ((0, 0), (0, 0), (0, t_dec_pad - t_dec)))
        c_row_s = _cumsum_rows(lf_all.reshape(dec_batch * h_a, t_dec_pad), 1.0)
        c_row_s = c_row_s.reshape(dec_batch, h_a, t_dec_pad)
        c_col_s = jnp.swapaxes(c_row_s, 1, 2)

        lams = [a[l].reshape(1, DH_B) for a in (lambda_q1, lambda_k1, lambda_q2, lambda_k2)]
        g_diff = diff_norm_g[l].reshape(1, w_b)

        oa_p = _fox_prompt(qat, ka, vat_b, c_row_p, aug_p,
                           batch=batch, seq=seq, tq=tq, tk=tk, n_heads=h_a)
        ob_p = _diff_prompt(qbt, kb, vbt_b, lams, g_diff, batch=batch, seq=seq, tq=tq, tk=tk,
                            n_heads=h_b, lambda_init=lambda_init)
        oa_s = _fox_decode(qa_s, kt_cache_a, vt_cache_a, ka_sb, va_sb, c_col_s, c_row_s,
                           layer=l, tq=dec_seq, tk=tk_dec, n_heads=h_a)
        ob_s = _diff_decode(qb_s, kt_cache_b, v_cache_b, kb_sb, vb_sb, lams, g_diff,
                            layer=l, tq=dec_seq, tk=tk_dec, n_heads=h_b, lambda_init=lambda_init)

        xp = _post(xp, oa_p, ob_p, wt_gate, *post_w, layer=l, tm=tm_p, alpha=alpha)
        xp = _ffn(xp, pp, *ffn_w, layer=l, tm=tm_p, alpha=alpha)
        xs = _post(xs, oa_s, ob_s, wt_gate, *post_w, layer=l, tm=tm_s, alpha=alpha)
        xs = _ffn(xs, ps, *ffn_w, layer=l, tm=tm_s, alpha=alpha)

        for lst, r in zip(new_s, (ka_s, va_s, logf_s, kb_s, vb_s)):
            lst.append(r)

    kat, vat, kbt, vb, logf_p = stacked
    ka_s, va_s, logf_s, kb_s, vb_s = [jnp.stack(a) for a in new_s]
    fox_k_p = jnp.transpose(kat.reshape(depth, batch, h_a, HD_A, seq), (0, 1, 4, 2, 3))
    fox_v_p = jnp.transpose(vat.reshape(depth, batch, h_a, HD_A, seq), (0, 1, 4, 2, 3))
    fox_lf_p = jnp.swapaxes(logf_p, 2, 3)
    diff_k_p = jnp.transpose(kbt.reshape(depth, batch, h_b, 2, DH_B, seq), (0, 1, 5, 2, 3, 4))
    return (xp.reshape(batch, seq, d_model), xs.reshape(dec_batch, dec_seq, d_model),
            fox_k_p, fox_v_p, fox_lf_p, diff_k_p, vb,
            ka_s.reshape(depth, dec_batch, dec_seq, h_a, HD_A),
            va_s.reshape(depth, dec_batch, dec_seq, h_a, HD_A),
            logf_s.reshape(depth, dec_batch, dec_seq, h_a),
            kb_s.reshape(depth, dec_batch, dec_seq, h_b, 2, DH_B),
            vb_s.reshape(depth, dec_batch, dec_seq, h_b, 2 * DH_B))
```
